```python
import jax, jax.numpy as jnp
from jax import lax
import numpy as np

D_MODEL = 2048
BATCH = 4
SEQ = 8192
DEPTH = 4
DEC_BATCH = 16
DEC_SEQ = 16
PAST_LEN = 1024

CHUNK = 64
A_HEAD_DIM = 128
A_HEADS = (D_MODEL // 2) // A_HEAD_DIM
A_KV_HEADS = 2
IDX_HEADS = 16
IDX_DIM = 64
IDX_W_SCALE = (IDX_HEADS ** -0.5) * (IDX_DIM ** -0.5)
TOPK_MAX = 256
ROPE_THETA = 10000.0
QBLOCK = 64
B_VAL_DIM = 128
B_HEADS = (D_MODEL // 4) // B_VAL_DIM
B_KEY_DIM = B_VAL_DIM // 2
B_GATE_RANK = 16
B_GATE_TEMP = 16.0
C_WINDOWS = (2, 4, 8, 16)
C_WIDTH = D_MODEL // 4
C_GROUP = C_WIDTH // len(C_WINDOWS)
POOL_HIST = max(C_WINDOWS) - 1
FFN_HIDDEN = -(-8 * D_MODEL // (3 * 256)) * 256
EPS = 1e-6

SPLIT_SIZES = (A_HEADS * A_HEAD_DIM, A_KV_HEADS * A_HEAD_DIM, A_KV_HEADS * A_HEAD_DIM,
               IDX_HEADS * IDX_DIM, IDX_DIM, IDX_HEADS,
               B_HEADS * B_KEY_DIM, B_HEADS * B_KEY_DIM, B_HEADS * B_VAL_DIM, B_GATE_RANK, B_HEADS * B_VAL_DIM,
               C_WIDTH)
PROJ_WIDTH = sum(SPLIT_SIZES)

kernel_name = "hymba_dsa_gla_pool_stream_step"

F32 = jnp.float32


def _rmsnorm(x, g):
    xf = x.astype(F32)
    y = xf * lax.rsqrt(jnp.mean(xf * xf, axis=-1, keepdims=True) + EPS)
    return (y * g.astype(F32)).astype(x.dtype)


def _rope(x, pos):
    half = x.shape[-1] // 2
    inv = ROPE_THETA ** (-jnp.arange(half, dtype=F32) / half)
    ang = pos.astype(F32)[:, None] * inv[None, :]
    cos = jnp.cos(ang)[:, None, :]
    sin = jnp.sin(ang)[:, None, :]
    xf = x.astype(F32)
    x1, x2 = xf[..., :half], xf[..., half:]
    return jnp.concatenate([x1 * cos - x2 * sin, x2 * cos + x1 * sin], axis=-1).astype(x.dtype)


def _sparse_block(q, qi, wi, limit, k, v, ki, topk):
    L = k.shape[1]
    rel = jax.nn.relu(jnp.einsum('bqhd,bsd->bqhs', qi.astype(F32), ki.astype(F32)))
    score = jnp.einsum('bqhs,bqh->bqs', rel, wi.astype(F32))
    vis = jnp.arange(L, dtype=jnp.int32)[None, :] < limit[:, None]
    score = jnp.where(vis[None], score, -jnp.inf)
    top_val, top_idx = lax.top_k(score, topk)
    valid = jnp.isfinite(top_val)
    gather = jax.vmap(lambda t, i: t[i])
    k_sel = gather(k, top_idx)
    v_sel = gather(v, top_idx)
    B, Q, H, D = q.shape
    qg = q.reshape(B, Q, A_KV_HEADS, H // A_KV_HEADS, D).astype(F32)
    s = jnp.einsum('bqhgd,bqnhd->bqhgn', qg, k_sel.astype(F32)) * (D ** -0.5)
    s = jnp.where(valid[:, :, None, None, :], s, -jnp.inf)
    p = jax.nn.softmax(s, axis=-1)
    o = jnp.einsum('bqhgn,bqnhd->bqhgd', p, v_sel.astype(F32))
    return o.reshape(B, Q, H * D).astype(q.dtype)


def _sparse_attn_prompt(q, qi, wi, k, v, ki):
    B, T = q.shape[0], q.shape[1]
    topk = min(TOPK_MAX, T // 4)
    nb = T // QBLOCK
    limit = ((jnp.arange(T, dtype=jnp.int32) // CHUNK) + 1) * CHUNK
    to_blocks = lambda a: jnp.moveaxis(a.reshape((B, nb, QBLOCK) + a.shape[2:]), 1, 0)

    def body(args):
        qb, qib, wib, lb = args
        return _sparse_block(qb, qib, wib, lb, k, v, ki, topk)

    out = lax.map(body, (to_blocks(q), to_blocks(qi), to_blocks(wi), limit.reshape(nb, QBLOCK)))
    return jnp.moveaxis(out, 0, 1).reshape(B, T, -1)


def _sparse_attn_sample(q, qi, wi, k, v, ki, ck, cv, cki):
    k_all = jnp.concatenate([ck, k], axis=1)
    v_all = jnp.concatenate([cv, v], axis=1)
    ki_all = jnp.concatenate([cki, ki], axis=1)
    L = k_all.shape[1]
    topk = min(TOPK_MAX, L // 4)
    limit = jnp.full((q.shape[1],), L, jnp.int32)
    return _sparse_block(q, qi, wi, limit, k_all, v_all, ki_all, topk)


def _gla_chunk(S, q, k, v, lg):
    q, k, v = q.astype(F32), k.astype(F32), v.astype(F32)
    C = q.shape[1]
    b = jnp.cumsum(lg, axis=1)
    causal = jnp.tril(jnp.ones((C, C), bool))
    diff = jnp.where(causal[None, :, :, None, None], b[:, :, None] - b[:, None, :], -jnp.inf)
    att = jnp.einsum('bthk,bshk,btshk->bhts', q, k, jnp.exp(diff))
    o = jnp.einsum('bhts,bshv->bthv', att, v) + jnp.einsum('bthk,bhkv->bthv', q * jnp.exp(b), S)
    b_last = b[:, -1]
    S_new = jnp.exp(b_last)[..., None] * S + jnp.einsum('bshk,bshv->bhkv', k * jnp.exp(b_last[:, None] - b), v)
    return S_new, o


def _gla_prompt(q, k, v, lg):
    B, T, H = q.shape[0], q.shape[1], q.shape[2]
    nc = T // CHUNK
    blk = lambda a: jnp.moveaxis(a.reshape((B, nc, CHUNK) + a.shape[2:]), 1, 0)
    S0 = jnp.zeros((B, H, B_KEY_DIM, B_VAL_DIM), F32)

    def step(S, xs):
        return _gla_chunk(S, *xs)

    S, o = lax.scan(step, S0, (blk(q), blk(k), blk(v), blk(lg)))
    return S, jnp.moveaxis(o, 0, 1).reshape(B, T, H, B_VAL_DIM)


def _pool_mix(u_ext, pos, pool_w, pool_scale):
    B = u_ext.shape[0]
    T = u_ext.shape[1] - POOL_HIST
    uf = u_ext.astype(F32)
    cs = jnp.concatenate([jnp.zeros((B, 1, C_WIDTH), F32), jnp.cumsum(uf, axis=1)], axis=1)
    end = cs[:, POOL_HIST + 1:]
    means = []
    for gi, w in enumerate(C_WINDOWS):
        sl = slice(gi * C_GROUP, (gi + 1) * C_GROUP)
        start = cs[:, POOL_HIST + 1 - w: POOL_HIST + 1 - w + T, sl]
        cnt = jnp.minimum(w, pos + 1).astype(F32)[None, :, None]
        means.append((end[..., sl] - start) / cnt)
    d = (jnp.concatenate(means, axis=-1) - uf[:, POOL_HIST:]).reshape(B, T, len(C_WINDOWS), C_GROUP)
    y = jnp.einsum('btgc,gcd->btgd', d, pool_w.astype(F32)).reshape(B, T, C_WIDTH) * pool_scale.astype(F32)
    return y.astype(u_ext.dtype)


def _layer(x, pos0, cache, prm):
    (g1, w_in, gq, gk, gki, w_alpha, b_alpha, g_gla, pool_w, pool_scale, w_o, g2, w_gate, w_up, w_down) = prm
    B, T, _ = x.shape
    pos = pos0 + jnp.arange(T, dtype=jnp.int32)
    n = _rmsnorm(x, g1)
    split_points = np.cumsum(SPLIT_SIZES)[:-1].tolist()
    (a_q, a_k, a_v, a_qi, a_ki, a_wi, b_q, b_k, b_v, b_a, b_g, c_u) = jnp.split(n @ w_in, split_points, axis=-1)
    q = _rope(_rmsnorm(a_q.reshape(B, T, A_HEADS, A_HEAD_DIM), gq), pos)
    k = _rope(_rmsnorm(a_k.reshape(B, T, A_KV_HEADS, A_HEAD_DIM), gk), pos)
    v = a_v.reshape(B, T, A_KV_HEADS, A_HEAD_DIM)
    qi = _rope(a_qi.reshape(B, T, IDX_HEADS, IDX_DIM), pos)
    ki = _rope(_rmsnorm(a_ki, gki)[:, :, None, :], pos)[:, :, 0]
    wi = a_wi * IDX_W_SCALE
    bq = b_q.reshape(B, T, B_HEADS, B_KEY_DIM) * (B_KEY_DIM ** -0.5)
    bk = b_k.reshape(B, T, B_HEADS, B_KEY_DIM)
    bv = b_v.reshape(B, T, B_HEADS, B_VAL_DIM)
    lg = jax.nn.log_sigmoid((b_a @ w_alpha + b_alpha).astype(F32)).reshape(B, T, B_HEADS, B_KEY_DIM) / B_GATE_TEMP
    if cache is None:
        o_a = _sparse_attn_prompt(q, qi, wi, k, v, ki)
        s_gla, o_b = _gla_prompt(bq, bk, bv, lg)
        u_ext = jnp.concatenate([jnp.zeros((B, POOL_HIST, C_WIDTH), c_u.dtype), c_u], axis=1)
    else:
        ck, cv, cki, c_gla, c_pool = cache
        o_a = _sparse_attn_sample(q, qi, wi, k, v, ki, ck, cv, cki)
        s_gla, o_b = _gla_chunk(c_gla.astype(F32), bq, bk, bv, lg)
        u_ext = jnp.concatenate([c_pool, c_u], axis=1)
    o_b = (_rmsnorm(o_b, g_gla).reshape(B, T, B_HEADS * B_VAL_DIM) * jax.nn.silu(b_g.astype(F32))).astype(x.dtype)
    o_c = _pool_mix(u_ext, pos, pool_w, pool_scale)
    h = x + jnp.concatenate([o_a, o_b, o_c], axis=-1) @ w_o
    m = _rmsnorm(h, g2)
    y = h + (jax.nn.silu(m @ w_gate) * (m @ w_up)) @ w_down
    return y, (k, v, ki, s_gla.astype(x.dtype), u_ext[:, -POOL_HIST:])


def setup_inputs(seed: int = 0) -> dict:
    key = jax.random.key(seed)
    ks = jax.random.split(key, 22)
    nrm = lambda kk, shape, scale: jax.random.normal(kk, shape, F32) * scale
    gain = lambda kk, shape: 1.0 + 0.02 * jax.random.normal(kk, shape, F32)
    HB = B_HEADS * B_KEY_DIM
    return {
        "x_prompt": nrm(ks[0], (BATCH, SEQ, D_MODEL), 1.0),
        "x_sample": nrm(ks[1], (DEC_BATCH, DEC_SEQ, D_MODEL), 1.0),
        "cache_k": nrm(ks[2], (DEPTH, DEC_BATCH, PAST_LEN, A_KV_HEADS, A_HEAD_DIM), 1.0),
        "cache_v": nrm(ks[3], (DEPTH, DEC_BATCH, PAST_LEN, A_KV_HEADS, A_HEAD_DIM), 1.0),
        "cache_kidx": nrm(ks[4], (DEPTH, DEC_BATCH, PAST_LEN, IDX_DIM), 1.0),
        "state_gla": nrm(ks[5], (DEPTH, DEC_BATCH, B_HEADS, B_KEY_DIM, B_VAL_DIM), 1.0),
        "state_pool": nrm(ks[6], (DEPTH, DEC_BATCH, POOL_HIST, C_WIDTH), 1.0),
        "norm1": gain(ks[7], (DEPTH, D_MODEL)),
        "w_in": nrm(ks[8], (DEPTH, D_MODEL, PROJ_WIDTH), D_MODEL ** -0.5),
        "q_norm": gain(ks[9], (DEPTH, A_HEAD_DIM)),
        "k_norm": gain(ks[10], (DEPTH, A_HEAD_DIM)),
        "kidx_norm": gain(ks[11], (DEPTH, IDX_DIM)),
        "w_alpha": nrm(ks[12], (DEPTH, B_GATE_RANK, HB), B_GATE_RANK ** -0.5),
        "b_alpha": nrm(ks[13], (DEPTH, HB), 0.02),
        "gla_norm": gain(ks[14], (DEPTH, B_VAL_DIM)),
        "pool_w": nrm(ks[15], (DEPTH, len(C_WINDOWS), C_GROUP, C_GROUP), C_GROUP ** -0.5),
        "pool_scale": gain(ks[16], (DEPTH, C_WIDTH)),
        "w_o": nrm(ks[17], (DEPTH, D_MODEL, D_MODEL), D_MODEL ** -0.5),
        "norm2": gain(ks[18], (DEPTH, D_MODEL)),
        "w_gate": nrm(ks[19], (DEPTH, D_MODEL, FFN_HIDDEN), D_MODEL ** -0.5),
        "w_up": nrm(ks[20], (DEPTH, D_MODEL, FFN_HIDDEN), D_MODEL ** -0.5),
        "w_down": nrm(ks[21], (DEPTH, FFN_HIDDEN, D_MODEL), FFN_HIDDEN ** -0.5),
    }


def reference(x_prompt, x_sample, cache_k, cache_v, cache_kidx, state_gla, state_pool,
              norm1, w_in, q_norm, k_norm, kidx_norm, w_alpha, b_alpha, gla_norm, pool_w, pool_scale,
              w_o, norm2, w_gate, w_up, w_down):
    yp, ys = x_prompt, x_sample
    st_p, st_s = [], []
    for l in range(DEPTH):
        prm = (norm1[l], w_in[l], q_norm[l], k_norm[l], kidx_norm[l], w_alpha[l], b_alpha[l], gla_norm[l],
               pool_w[l], pool_scale[l], w_o[l], norm2[l], w_gate[l], w_up[l], w_down[l])
        yp, sp = _layer(yp, 0, None, prm)
        ys, ss = _layer(ys, PAST_LEN, (cache_k[l], cache_v[l], cache_kidx[l], state_gla[l], state_pool[l]), prm)
        st_p.append(sp)
        st_s.append(ss)
    stk = lambda sts, i: jnp.stack([s[i] for s in sts], axis=0)
    return (yp, ys,
            stk(st_p, 0), stk(st_p, 1), stk(st_p, 2), stk(st_p, 3), stk(st_p, 4),
            stk(st_s, 0), stk(st_s, 1), stk(st_s, 2), stk(st_s, 3), stk(st_s, 4))
```

```python
import functools

import numpy as np
import jax
import jax.numpy as jnp
from jax import lax
from jax.experimental import pallas as pl
from jax.experimental.pallas import tpu as pltpu

F32 = jnp.float32
BF16 = jnp.bfloat16

EPS = 1e-6
CHUNK = 64
TOPK_MAX = 256
ROPE_THETA = 10000.0
A_HEAD_DIM = 128
A_HEADS = 8
A_KV_HEADS = 2
IDX_HEADS = 16
IDX_DIM = 64
IDX_W_SCALE = (IDX_HEADS ** -0.5) * (IDX_DIM ** -0.5)
B_HEADS = 4
B_KEY_DIM = 64
B_VAL_DIM = 128
B_GATE_RANK = 16
B_GATE_TEMP = 16.0
C_WINDOWS = (2, 4, 8, 16)
C_WIDTH = 512
C_GROUP = 128
POOL_HIST = 15
HIST_ROWS = 16

LANES = 128
VMEM_LIMIT = 56 * 1024 * 1024

INT_MIN = -(2 ** 31)

_SRC_SIZES = (("a_q", 1024), ("a_k", 256), ("a_v", 256), ("a_qi", 1024), ("a_ki", 64), ("a_wi", 16),
              ("b_q", 256), ("b_k", 256), ("b_v", 512), ("b_a", 16), ("b_g", 512), ("c_u", 512))
_PACK_ORDER = ("a_q", "a_qi", "b_v", "b_g", "c_u", "a_k", "a_v", "b_q", "b_k", "a_ki", "a_wi", "b_a")


def _build_layout():
    src, off = {}, 0
    for name, w in _SRC_SIZES:
        src[name] = (off, w)
        off += w
    lay, poff = {}, 0
    for name in _PACK_ORDER:
        so, w = src[name]
        pw = -(-w // LANES) * LANES
        lay[name] = (so, w, poff, pw)
        poff += pw
    return lay, poff


_LAYOUT, PACKED_WIDTH = _build_layout()


def _seg(name):
    _, _, poff, pw = _LAYOUT[name]
    return poff, pw


def _dot(a, b):
    return jnp.dot(a, b, preferred_element_type=F32)


def _dot_nt(a, b):
    return lax.dot_general(a, b, (((1,), (1,)), ((), ())), preferred_element_type=F32)


def _dot_tn(a, b):
    return lax.dot_general(a, b, (((0,), (0,)), ((), ())), preferred_element_type=F32)


def _cparams(sem):
    return pltpu.CompilerParams(dimension_semantics=sem, vmem_limit_bytes=VMEM_LIMIT)


def _norm_matmul_kernel(x_ref, g_ref, w_ref, o_ref, xn_ref):
    @pl.when(pl.program_id(1) == 0)
    def _():
        x = x_ref[...]
        ms = jnp.mean(x * x, axis=-1, keepdims=True)
        xn_ref[...] = (x * lax.rsqrt(ms + EPS) * g_ref[...]).astype(BF16)

    o_ref[...] = _dot(xn_ref[...], w_ref[...])


def _norm_matmul(x, g, w, bm, bn):
    M, D = x.shape
    N = w.shape[1]
    return pl.pallas_call(
        _norm_matmul_kernel,
        out_shape=jax.ShapeDtypeStruct((M, N), F32),
        grid=(M // bm, N // bn),
        in_specs=[pl.BlockSpec((bm, D), lambda i, j: (i, 0)),
                  pl.BlockSpec((1, D), lambda i, j: (0, 0)),
                  pl.BlockSpec((D, bn), lambda i, j: (0, j))],
        out_specs=pl.BlockSpec((bm, bn), lambda i, j: (i, j)),
        scratch_shapes=[pltpu.VMEM((bm, D), BF16)],
        compiler_params=_cparams(("arbitrary", "arbitrary")),
        name="norm_matmul",
    )(x, g.reshape(1, D), w)


def _post_kernel(p_ref, cosa_ref, sina_ref, cosb_ref, sinlo_ref, sinhi_ref, gq_ref, gk_ref, gki_ref,
                 walpha_ref, balpha_ref, poolw_ref, pscale_ref, hist0_ref,
                 q_ref, qi_ref, k32_ref, kb_ref, vb_ref, ki32_ref, kieo_ref, wt_ref,
                 bq_ref, bk_ref, bv_ref, lg_ref, gate_ref, oc_ref,
                 uext_ref, hist_ref, *, tt, pos0):
    i = pl.program_id(1)

    def seg(name):
        off, w = _seg(name)
        return p_ref[:, off:off + w]

    cosa, sina = cosa_ref[...], sina_ref[...]

    def head_norm_rope(a, g):
        r = lax.rsqrt(jnp.mean(a * a, axis=-1, keepdims=True) + EPS)
        y = a * r * g
        return y * cosa + pltpu.roll(y, 64, 1) * sina

    aq = seg("a_q")
    gq = gq_ref[...]
    for h in range(A_HEADS):
        sl = slice(h * LANES, (h + 1) * LANES)
        q_ref[:, sl] = (head_norm_rope(aq[:, sl], gq) * (A_HEAD_DIM ** -0.5)).astype(BF16)
    ak = seg("a_k")
    gk = gk_ref[...]
    for h in range(A_KV_HEADS):
        sl = slice(h * LANES, (h + 1) * LANES)
        kk = head_norm_rope(ak[:, sl], gk)
        k32_ref[:, sl] = kk
        kb_ref[:, sl] = kk.astype(BF16)
    vb_ref[...] = seg("a_v").astype(BF16)

    cosb, sinlo, sinhi = cosb_ref[...], sinlo_ref[...], sinhi_ref[...]

    def rope64(y):
        return y * cosb + pltpu.roll(y, 96, 1) * sinlo + pltpu.roll(y, 32, 1) * sinhi

    aqi = seg("a_qi")
    for gidx in range(IDX_HEADS // 2):
        sl = slice(gidx * LANES, (gidx + 1) * LANES)
        qi_ref[:, sl] = rope64(aqi[:, sl]).astype(BF16)
    aki = seg("a_ki")
    rki = lax.rsqrt(jnp.sum(aki * aki, axis=-1, keepdims=True) * (1.0 / IDX_DIM) + EPS)
    ki = rope64(aki * rki * gki_ref[...])
    ki32_ref[...] = ki[:, :IDX_DIM]
    kieo_ref[:, 0:LANES] = ki.astype(BF16)
    kieo_ref[:, LANES:2 * LANES] = pltpu.roll(ki, 64, 1).astype(BF16)

    awi = seg("a_wi") * IDX_W_SCALE
    if tt % LANES == 0:
        for c in range(tt // LANES):
            wt_ref[:, c * LANES:(c + 1) * LANES] = awi[c * LANES:(c + 1) * LANES, :].T[:IDX_HEADS, :]
    else:
        padded = jnp.concatenate([awi, jnp.zeros((LANES - tt, LANES), F32)], axis=0)
        wt_ref[...] = padded.T[:IDX_HEADS, :tt]

    bq_ref[...] = seg("b_q") * (B_KEY_DIM ** -0.5)
    bk_ref[...] = seg("b_k")
    bv_ref[...] = seg("b_v").astype(BF16)
    z = _dot(seg("b_a").astype(BF16), walpha_ref[...]) + balpha_ref[...]
    lg_ref[...] = (jnp.minimum(z, 0.0) - jnp.log1p(jnp.exp(-jnp.abs(z)))) * (1.0 / B_GATE_TEMP)
    bg = seg("b_g")
    gate_ref[...] = bg * (1.0 / (1.0 + jnp.exp(-bg)))

    @pl.when(i == 0)
    def _():
        hist_ref[...] = hist0_ref[...]

    u = seg("c_u")
    uext_ref[0:HIST_ROWS, :] = hist_ref[...]
    uext_ref[HIST_ROWS:HIST_ROWS + tt, :] = u
    hist_ref[...] = uext_ref[tt:tt + HIST_ROWS, :]
    pos = pos0 + i * tt + lax.broadcasted_iota(jnp.int32, (tt, 1), 0)
    for gi, w in enumerate(C_WINDOWS):
        sl = slice(gi * C_GROUP, (gi + 1) * C_GROUP)
        s = uext_ref[:, sl]
        shift = 1
        while shift < w:
            s = s + pltpu.roll(s, shift, 0)
            shift *= 2
        cnt = jnp.minimum(w, pos + 1).astype(F32)
        d = s[HIST_ROWS:, :] / cnt - u[:, sl]
        y = _dot(d.astype(BF16), poolw_ref[gi]) * pscale_ref[:, sl]
        oc_ref[:, sl] = y.astype(BF16)


def _post(P, tabs, gq, gk, gki, w_alpha, b_alpha, pool_w, pool_scale, hist0, tt, pos0):
    B, T, W = P.shape
    nT = T // tt
    tok = lambda c: pl.BlockSpec((None, tt, c), lambda b, i: (b, i, 0))
    tab = pl.BlockSpec((tt, LANES), lambda b, i: (i, 0))
    full = lambda shape: pl.BlockSpec(shape, lambda b, i: (0,) * len(shape))
    out_shape = [
        jax.ShapeDtypeStruct((B, T, 1024), BF16),
        jax.ShapeDtypeStruct((B, T, 1024), BF16),
        jax.ShapeDtypeStruct((B, T, 256), F32),
        jax.ShapeDtypeStruct((B, T, 256), BF16),
        jax.ShapeDtypeStruct((B, T, 256), BF16),
        jax.ShapeDtypeStruct((B, T, IDX_DIM), F32),
        jax.ShapeDtypeStruct((B, T, 256), BF16),
        jax.ShapeDtypeStruct((B, IDX_HEADS, T), F32),
        jax.ShapeDtypeStruct((B, T, 256), F32),
        jax.ShapeDtypeStruct((B, T, 256), F32),
        jax.ShapeDtypeStruct((B, T, 512), BF16),
        jax.ShapeDtypeStruct((B, T, 256), F32),
        jax.ShapeDtypeStruct((B, T, 512), F32),
        jax.ShapeDtypeStruct((B, T, 512), BF16),
    ]
    out_specs = [tok(1024), tok(1024), tok(256), tok(256), tok(256), tok(IDX_DIM), tok(256),
                 pl.BlockSpec((None, IDX_HEADS, tt), lambda b, i: (b, 0, i)),
                 tok(256), tok(256), tok(512), tok(256), tok(512), tok(512)]
    return pl.pallas_call(
        functools.partial(_post_kernel, tt=tt, pos0=pos0),
        out_shape=out_shape,
        grid=(B, nT),
        in_specs=[tok(W), tab, tab, tab, tab, tab,
                  full((1, LANES)), full((1, LANES)), full((1, LANES)),
                  full((LANES, 256)), full((1, 256)),
                  full((len(C_WINDOWS), C_GROUP, C_GROUP)), full((1, C_WIDTH)),
                  pl.BlockSpec((None, HIST_ROWS, C_WIDTH), lambda b, i: (b, 0, 0))],
        out_specs=out_specs,
        scratch_shapes=[pltpu.VMEM((tt + HIST_ROWS, C_WIDTH), F32), pltpu.VMEM((HIST_ROWS, C_WIDTH), F32)],
        compiler_params=_cparams(("arbitrary", "arbitrary")),
        name="post",
    )(P, *tabs, gq, gk, gki, w_alpha, b_alpha, pool_w, pool_scale, hist0)


def _attn_kernel(q_ref, qi_ref, wt_ref, k_ref, v_ref, ki_ref, o_ref,
                 key_ref, acc_ref, m_ref, l_ref, *, tq, kb_size, causal, n_keys, topk):
    i = pl.program_id(1)
    n_slabs = IDX_HEADS // 2
    if causal:
        nkb = ((i + 1) * tq + kb_size - 1) // kb_size
        t_abs = i * tq + lax.broadcasted_iota(jnp.int32, (1, tq), 1)
        limit = (t_abs // CHUNK + 1) * CHUNK
    else:
        nkb = -(-n_keys // kb_size)
        limit = jnp.full((1, tq), n_keys, jnp.int32)

    q2 = jnp.concatenate([qi_ref[:, g * LANES:(g + 1) * LANES] for g in range(n_slabs)], axis=0)

    def idx_body(kb, carry):
        off = pl.multiple_of(kb * kb_size, kb_size)
        ye = _dot_nt(ki_ref[pl.ds(off, kb_size), 0:LANES], q2)
        yo = _dot_nt(ki_ref[pl.ds(off, kb_size), LANES:2 * LANES], q2)
        score = jnp.zeros((kb_size, tq), F32)
        for g in range(n_slabs):
            score = score + jnp.maximum(ye[:, g * tq:(g + 1) * tq], 0.0) * wt_ref[2 * g:2 * g + 1, :]
            score = score + jnp.maximum(yo[:, g * tq:(g + 1) * tq], 0.0) * wt_ref[2 * g + 1:2 * g + 2, :]
        s_idx = off + lax.broadcasted_iota(jnp.int32, (kb_size, 1), 0)
        bits = pltpu.bitcast(score, jnp.int32)
        key = bits ^ ((bits >> 31) & jnp.int32(0x7FFFFFFF))
        key_ref[pl.ds(off, kb_size), :] = jnp.where(s_idx < limit, key, jnp.int32(INT_MIN))
        return carry

    lax.fori_loop(0, nkb, idx_body, 0)

    def bit_body(bi, ut):
        cand = ut | lax.shift_left(jnp.int32(1), 31 - bi)
        cand_s = cand ^ jnp.int32(INT_MIN)

        def cnt_body(kb, acc):
            off = pl.multiple_of(kb * kb_size, kb_size)
            ge = jnp.where(key_ref[pl.ds(off, kb_size), :] >= cand_s, 1, 0).astype(jnp.int32)
            return acc + jnp.sum(ge.reshape(kb_size // 8, 8, tq), axis=0)

        acc = lax.fori_loop(0, nkb, cnt_body, jnp.zeros((8, tq), jnp.int32))
        cnt = jnp.sum(acc, axis=0, keepdims=True)
        return jnp.where(cnt >= topk, cand, ut)

    ut = lax.fori_loop(0, 32, bit_body, jnp.zeros((1, tq), jnp.int32))
    tau = jnp.maximum(ut ^ jnp.int32(INT_MIN), jnp.int32(INT_MIN + 1))

    hpg = A_HEADS // A_KV_HEADS
    for g in range(A_KV_HEADS):
        qg = jnp.concatenate([q_ref[:, (g * hpg + h) * LANES:(g * hpg + h + 1) * LANES] for h in range(hpg)], axis=0)
        m_ref[...] = jnp.full(m_ref.shape, -1e30, F32)
        l_ref[...] = jnp.zeros(l_ref.shape, F32)
        acc_ref[...] = jnp.zeros(acc_ref.shape, F32)

        def att_body(kb, carry):
            off = pl.multiple_of(kb * kb_size, kb_size)
            s = _dot_nt(k_ref[pl.ds(off, kb_size), g * LANES:(g + 1) * LANES], qg)
            sel = key_ref[pl.ds(off, kb_size), :] >= tau
            s = jnp.concatenate([jnp.where(sel, s[:, h * tq:(h + 1) * tq], -jnp.inf) for h in range(hpg)], axis=1)
            m_old = m_ref[...]
            m_new = jnp.maximum(m_old, jnp.max(s, axis=0, keepdims=True))
            alpha = jnp.exp(m_old - m_new)
            p = jnp.exp(s - m_new)
            l_ref[...] = alpha * l_ref[...] + jnp.sum(p, axis=0, keepdims=True)
            pv = _dot_tn(v_ref[pl.ds(off, kb_size), g * LANES:(g + 1) * LANES], p.astype(BF16))
            acc_ref[...] = acc_ref[...] * alpha + pv
            m_ref[...] = m_new
            return carry

        lax.fori_loop(0, nkb, att_body, 0)
        o = acc_ref[...] / l_ref[...]
        for h in range(hpg):
            hh = g * hpg + h
            o_ref[:, hh * LANES:(hh + 1) * LANES] = o[:, h * tq:(h + 1) * tq].T.astype(o_ref.dtype)


def _attn(q, qi, wt, kb, vb, kieo, tq, kb_size, causal, n_keys, topk):
    B, T, _ = q.shape
    S = kb.shape[1]
    hpg = A_HEADS // A_KV_HEADS
    kern = functools.partial(_attn_kernel, tq=tq, kb_size=kb_size, causal=causal, n_keys=n_keys, topk=topk)
    return pl.pallas_call(
        kern,
        out_shape=jax.ShapeDtypeStruct((B, T, 1024), BF16),
        grid=(B, T // tq),
        in_specs=[pl.BlockSpec((None, tq, 1024), lambda b, i: (b, i, 0)),
                  pl.BlockSpec((None, tq, 1024), lambda b, i: (b, i, 0)),
                  pl.BlockSpec((None, IDX_HEADS, tq), lambda b, i: (b, 0, i)),
                  pl.BlockSpec((None, S, 256), lambda b, i: (b, 0, 0)),
                  pl.BlockSpec((None, S, 256), lambda b, i: (b, 0, 0)),
                  pl.BlockSpec((None, S, 256), lambda b, i: (b, 0, 0))],
        out_specs=pl.BlockSpec((None, tq, 1024), lambda b, i: (b, i, 0)),
        scratch_shapes=[pltpu.VMEM((S, tq), jnp.int32),
                        pltpu.VMEM((A_HEAD_DIM, hpg * tq), F32),
                        pltpu.VMEM((1, hpg * tq), F32),
                        pltpu.VMEM((1, hpg * tq), F32)],
        compiler_params=_cparams(("arbitrary", "arbitrary")),
        name="attn",
    )(q, qi, wt, kb, vb, kieo)


def _gla_consts(C):
    nl = int(np.log2(C))
    M = np.zeros((nl + 2, C, C), np.float32)
    masks = np.zeros((nl + 1, C, C), np.float32)
    for j in range(nl):
        m = 1 << j
        for t in range(C):
            blk = (t // (2 * m)) * 2 * m
            mid = blk + m
            if t >= mid:
                M[j, t, mid:t + 1] = 1
                masks[j, t, blk:mid] = 1
            else:
                M[j, t, t + 1:mid] = 1
    for t in range(C):
        M[nl, t, :t + 1] = 1
        M[nl + 1, t, t + 1:] = 1
    masks[nl] = np.eye(C)
    return M.reshape((nl + 2) * C, C), np.concatenate([masks, masks], axis=1)


def _gla_kernel(q_ref, k_ref, v_ref, lg_ref, gate_ref, mcat_ref, masks_ref, gn_ref, s0_ref,
                o_ref, sout_ref, s_ref, *, tt, C):
    i = pl.program_id(1)
    nl = int(np.log2(C))

    @pl.when(i == 0)
    def _():
        s_ref[...] = s0_ref[...]

    lane = lax.broadcasted_iota(jnp.int32, (1, LANES), 1)
    low = lane < B_KEY_DIM
    row = lax.broadcasted_iota(jnp.int32, (C, 1), 0)
    rr = lax.broadcasted_iota(jnp.int32, (LANES, 2 * B_VAL_DIM), 0)
    cc = lax.broadcasted_iota(jnp.int32, (LANES, 2 * B_VAL_DIM), 1)
    blockdiag = (rr < B_KEY_DIM) == (cc < B_VAL_DIM)
    gn = gn_ref[...]

    def stack2(x):
        return jnp.concatenate([jnp.where(low, x, 0.0), jnp.where(low, 0.0, x)], axis=0).astype(BF16)

    def chunk_body(c, carry):
        r0 = pl.multiple_of(c * C, C)
        qc = q_ref[pl.ds(r0, C), :]
        kc = k_ref[pl.ds(r0, C), :]
        vc = v_ref[pl.ds(r0, C), :]
        lgc = lg_ref[pl.ds(r0, C), :]
        hi = lgc.astype(BF16)
        r1 = lgc - hi.astype(F32)
        mid = r1.astype(BF16)
        lo = (r1 - mid.astype(F32)).astype(BF16)
        mcat = mcat_ref[...]
        E = jnp.exp(_dot(mcat, hi) + _dot(mcat, mid) + _dot(mcat, lo))
        e_b = E[nl * C:(nl + 1) * C, :]
        e_k = E[(nl + 1) * C:(nl + 2) * C, :]
        qt = qc * e_b
        kt = kc * e_k
        e_last = e_b[C - 1:C, :]
        for g in range(B_HEADS // 2):
            sl = slice(g * LANES, (g + 1) * LANES)
            att = _dot_nt(stack2(qc[:, sl]), kc[:, sl].astype(BF16)) * masks_ref[nl]
            for j in range(nl):
                right = ((row >> j) & 1) == 1
                x = jnp.where(right, qc[:, sl], kc[:, sl]) * E[j * C:(j + 1) * C, sl]
                att = att + _dot_nt(stack2(x), x.astype(BF16)) * masks_ref[j]
            att = att.astype(BF16)
            vg = vc[:, g * 2 * B_VAL_DIM:(g + 1) * 2 * B_VAL_DIM]
            sg = s_ref[g]
            o_intra = jnp.concatenate([_dot(att[0:C], vg[:, 0:B_VAL_DIM]),
                                       _dot(att[C:2 * C], vg[:, B_VAL_DIM:2 * B_VAL_DIM])], axis=1)
            o = o_intra + _dot(qt[:, sl].astype(BF16), sg.astype(BF16))
            upd = _dot_tn(kt[:, sl].astype(BF16), vg)
            dec = jnp.broadcast_to(e_last[:, sl], (LANES, LANES)).T
            dec = jnp.concatenate([dec, dec], axis=1)
            s_ref[g] = dec * sg + jnp.where(blockdiag, upd, 0.0)
            for hh in range(2):
                h = 2 * g + hh
                oh = o[:, hh * B_VAL_DIM:(hh + 1) * B_VAL_DIM]
                r = lax.rsqrt(jnp.mean(oh * oh, axis=-1, keepdims=True) + EPS)
                gsl = slice(h * B_VAL_DIM, (h + 1) * B_VAL_DIM)
                o_ref[pl.ds(r0, C), gsl] = (oh * r * gn * gate_ref[pl.ds(r0, C), gsl]).astype(o_ref.dtype)
        return carry

    lax.fori_loop(0, tt // C, chunk_body, 0)

    @pl.when(i == pl.num_programs(1) - 1)
    def _():
        sout_ref[...] = s_ref[...]


def _gla(bq, bk, bv, lg, gate, gn, s0, tt, C):
    B, T, _ = bq.shape
    mcat, masks = _gla_consts(C)
    nl = int(np.log2(C))
    tok = lambda c: pl.BlockSpec((None, tt, c), lambda b, i: (b, i, 0))
    full = lambda shape: pl.BlockSpec(shape, lambda b, i: (0,) * len(shape))
    st = pl.BlockSpec((None, 2, LANES, 2 * B_VAL_DIM), lambda b, i: (b, 0, 0, 0))
    return pl.pallas_call(
        functools.partial(_gla_kernel, tt=tt, C=C),
        out_shape=[jax.ShapeDtypeStruct((B, T, 512), BF16),
                   jax.ShapeDtypeStruct((B, 2, LANES, 2 * B_VAL_DIM), F32)],
        grid=(B, T // tt),
        in_specs=[tok(256), tok(256), tok(512), tok(256), tok(512),
                  full(((nl + 2) * C, C)), full((nl + 1, 2 * C, C)), full((1, B_VAL_DIM)), st],
        out_specs=[tok(512), st],
        scratch_shapes=[pltpu.VMEM((2, LANES, 2 * B_VAL_DIM), F32)],
        compiler_params=_cparams(("arbitrary", "arbitrary")),
        name="gla",
    )(bq, bk, bv, lg, gate, jnp.asarray(mcat, BF16), jnp.asarray(masks, F32), gn, s0)


def _out_kernel(x_ref, oa_ref, ob_ref, oc_ref, wo_ref, g2_ref, h_ref, m_ref):
    acc = _dot(oa_ref[...], wo_ref[0:1024, :])
    acc = acc + _dot(ob_ref[...], wo_ref[1024:1536, :])
    acc = acc + _dot(oc_ref[...], wo_ref[1536:2048, :])
    h = x_ref[...] + acc
    h_ref[...] = h
    ms = jnp.mean(h * h, axis=-1, keepdims=True)
    m_ref[...] = (h * lax.rsqrt(ms + EPS) * g2_ref[...]).astype(BF16)


def _out_proj(x, oa, ob, oc, wo, g2, bm):
    M, D = x.shape
    row = lambda c: pl.BlockSpec((bm, c), lambda i: (i, 0))
    return pl.pallas_call(
        _out_kernel,
        out_shape=[jax.ShapeDtypeStruct((M, D), F32), jax.ShapeDtypeStruct((M, D), BF16)],
        grid=(M // bm,),
        in_specs=[row(D), row(1024), row(512), row(512),
                  pl.BlockSpec((D, D), lambda i: (0, 0)), pl.BlockSpec((1, D), lambda i: (0, 0))],
        out_specs=[row(D), row(D)],
        compiler_params=_cparams(("arbitrary",)),
        name="out_proj",
    )(x, oa, ob, oc, wo, g2.reshape(1, D))


def _ffn_kernel(h_ref, m_ref, wg_ref, wu_ref, wd_ref, y_ref):
    @pl.when(pl.program_id(1) == 0)
    def _():
        y_ref[...] = h_ref[...]

    m = m_ref[...]
    gate = _dot(m, wg_ref[...])
    up = _dot(m, wu_ref[...])
    a = (gate * (1.0 / (1.0 + jnp.exp(-gate))) * up).astype(BF16)
    y_ref[...] += _dot(a, wd_ref[...])


def _ffn(h, m, wg, wu, wd, bm, bf):
    M, D = h.shape
    Fh = wg.shape[1]
    return pl.pallas_call(
        _ffn_kernel,
        out_shape=jax.ShapeDtypeStruct((M, D), F32),
        grid=(M // bm, Fh // bf),
        in_specs=[pl.BlockSpec((bm, D), lambda i, j: (i, 0)),
                  pl.BlockSpec((bm, D), lambda i, j: (i, 0)),
                  pl.BlockSpec((D, bf), lambda i, j: (0, j)),
                  pl.BlockSpec((D, bf), lambda i, j: (0, j)),
                  pl.BlockSpec((bf, D), lambda i, j: (j, 0))],
        out_specs=pl.BlockSpec((bm, D), lambda i, j: (i, 0)),
        compiler_params=_cparams(("arbitrary", "arbitrary")),
        name="ffn",
    )(h, m, wg, wu, wd)


def _rope_tables(pos0, T):
    pos = (pos0 + jnp.arange(T, dtype=jnp.int32)).astype(F32)

    def cs(half):
        inv = ROPE_THETA ** (-jnp.arange(half, dtype=F32) / half)
        ang = pos[:, None] * inv[None, :]
        return jnp.cos(ang), jnp.sin(ang)

    c64, s64 = cs(A_HEAD_DIM // 2)
    c32, s32 = cs(IDX_DIM // 2)
    z32 = jnp.zeros_like(s32)
    return (jnp.concatenate([c64, c64], axis=1), jnp.concatenate([-s64, s64], axis=1),
            jnp.concatenate([c32, c32, c32, c32], axis=1),
            jnp.concatenate([-s32, z32, -s32, z32], axis=1),
            jnp.concatenate([z32, s32, z32, s32], axis=1))


def _pack_w_in(w_in):
    cols = []
    for name in _PACK_ORDER:
        so, w, _, pw = _LAYOUT[name]
        piece = w_in[:, :, so:so + w]
        if pw > w:
            piece = jnp.pad(piece, ((0, 0), (0, 0), (0, pw - w)))
        cols.append(piece)
    return jnp.concatenate(cols, axis=-1).astype(BF16)


def _state_to_groups(s):
    B = s.shape[0]
    s = s.reshape(B, 2, 2, B_KEY_DIM, B_VAL_DIM)
    z = jnp.zeros_like(s[:, :, 0])
    top = jnp.concatenate([s[:, :, 0], z], axis=-1)
    bot = jnp.concatenate([z, s[:, :, 1]], axis=-1)
    return jnp.concatenate([top, bot], axis=-2)


def _groups_to_state(sg):
    B = sg.shape[0]
    h0 = sg[:, :, :B_KEY_DIM, :B_VAL_DIM]
    h1 = sg[:, :, B_KEY_DIM:, B_VAL_DIM:]
    return jnp.stack([h0, h1], axis=2).reshape(B, B_HEADS, B_KEY_DIM, B_VAL_DIM)


def _layer(x, pos0, cache, prm):
    (g1, w_in_p, gq, gk, gki, w_alpha, b_alpha, g_gla, pool_w, pool_scale, w_o, g2, w_gate, w_up, w_down) = prm
    B, T, D = x.shape
    M = B * T
    x2 = x.reshape(M, D)
    bm = min(512, M)
    P = _norm_matmul(x2, g1, w_in_p, bm, PACKED_WIDTH // 3).reshape(B, T, PACKED_WIDTH)

    tt = min(512, T)
    tabs = _rope_tables(pos0, T)
    gki_p = jnp.pad(gki, (0, LANES - IDX_DIM)).reshape(1, LANES)
    w_alpha_p = jnp.pad(w_alpha, ((0, LANES - B_GATE_RANK), (0, 0))).astype(BF16)
    if cache is None:
        hist0 = jnp.zeros((B, HIST_ROWS, C_WIDTH), F32)
    else:
        hist0 = jnp.pad(cache[4], ((0, 0), (HIST_ROWS - POOL_HIST, 0), (0, 0)))
    (q, qi, k32, kb, vb, ki32, kieo, wt, bq, bk, bv, lg, gate, oc) = _post(
        P, tabs, gq.reshape(1, LANES), gk.reshape(1, LANES), gki_p, w_alpha_p, b_alpha.reshape(1, -1),
        pool_w.astype(BF16), pool_scale.reshape(1, -1), hist0, tt, pos0)

    v_off, _ = _seg("a_v")
    v32 = P[:, :, v_off:v_off + 256]
    u_off, _ = _seg("c_u")
    pool_state = P[:, T - POOL_HIST:, u_off:u_off + C_WIDTH]

    if cache is None:
        tq = LANES
        oa = _attn(q, qi, wt, kb, vb, kieo, tq, 512, True, T, min(TOPK_MAX, T // 4))
        s0 = jnp.zeros((B, 2, LANES, 2 * B_VAL_DIM), F32)
        ob, sg = _gla(bq, bk, bv, lg, gate, g_gla.reshape(1, -1), s0, tt, CHUNK)
    else:
        ck, cv, cki, c_gla, _ = cache
        past = ck.shape[1]
        L = past + T
        kb_size = 384
        S = -(-L // kb_size) * kb_size
        padk = lambda a: jnp.pad(a, ((0, 0), (0, S - L), (0, 0)))
        k_all = padk(jnp.concatenate([ck.reshape(B, past, 256).astype(BF16), kb], axis=1))
        v_all = padk(jnp.concatenate([cv.reshape(B, past, 256).astype(BF16), vb], axis=1))
        zc = jnp.zeros((B, past, IDX_DIM), BF16)
        cki_b = cki.astype(BF16)
        ki_all = padk(jnp.concatenate([jnp.concatenate([cki_b, zc, zc, cki_b], axis=-1), kieo], axis=1))
        tq = LANES
        padq = lambda a: jnp.pad(a, ((0, 0), (0, tq - T), (0, 0)))
        wt_p = jnp.pad(wt, ((0, 0), (0, 0), (0, tq - T)))
        oa = _attn(padq(q), padq(qi), wt_p, k_all, v_all, ki_all, tq, kb_size, False, L,
                   min(TOPK_MAX, L // 4))[:, :T]
        ob, sg = _gla(bq, bk, bv, lg, gate, g_gla.reshape(1, -1), _state_to_groups(c_gla), tt, T)

    h, m = _out_proj(x2, oa.reshape(M, -1), ob.reshape(M, -1), oc.reshape(M, -1), w_o, g2, bm)
    y = _ffn(h, m, w_gate, w_up, w_down, bm, 512)
    state = (k32.reshape(B, T, A_KV_HEADS, A_HEAD_DIM), v32.reshape(B, T, A_KV_HEADS, A_HEAD_DIM),
             ki32, _groups_to_state(sg), pool_state)
    return y.reshape(B, T, D), state


def kernel(x_prompt, x_sample, cache_k, cache_v, cache_kidx, state_gla, state_pool,
           norm1, w_in, q_norm, k_norm, kidx_norm, w_alpha, b_alpha, gla_norm, pool_w, pool_scale,
           w_o, norm2, w_gate, w_up, w_down):
    depth = w_in.shape[0]
    past = cache_k.shape[2]
    w_in_p = _pack_w_in(w_in)
    w_o_b, w_gate_b, w_up_b, w_down_b = (w.astype(BF16) for w in (w_o, w_gate, w_up, w_down))
    yp, ys = x_prompt, x_sample
    st_p, st_s = [], []
    for l in range(depth):
        prm = (norm1[l], w_in_p[l], q_norm[l], k_norm[l], kidx_norm[l], w_alpha[l], b_alpha[l], gla_norm[l],
               pool_w[l], pool_scale[l], w_o_b[l], norm2[l], w_gate_b[l], w_up_b[l], w_down_b[l])
        yp, sp = _layer(yp, 0, None, prm)
        ys, ss = _layer(ys, past, (cache_k[l], cache_v[l], cache_kidx[l], state_gla[l], state_pool[l]), prm)
        st_p.append(sp)
        st_s.append(ss)
    stk = lambda sts, i: jnp.stack([s[i] for s in sts], axis=0)
    return (yp, ys,
            stk(st_p, 0), stk(st_p, 1), stk(st_p, 2), stk(st_p, 3), stk(st_p, 4),
            stk(st_s, 0), stk(st_s, 1), stk(st_s, 2), stk(st_s, 3), stk(st_s, 4))
```

```python
import functools

import numpy as np
import jax
import jax.numpy as jnp
from jax import lax
from jax.experimental import pallas as pl
from jax.experimental.pallas import tpu as pltpu

F32 = jnp.float32
BF16 = jnp.bfloat16

EPS = 1e-6
CHUNK = 64
TOPK_MAX = 256
ROPE_THETA = 10000.0
A_HEAD_DIM = 128
A_HEADS = 8
A_KV_HEADS = 2
IDX_HEADS = 16
IDX_DIM = 64
IDX_W_SCALE = (IDX_HEADS ** -0.5) * (IDX_DIM ** -0.5)
B_HEADS = 4
B_KEY_DIM = 64
B_VAL_DIM = 128
B_GATE_RANK = 16
B_GATE_TEMP = 16.0
C_WINDOWS = (2, 4, 8, 16)
C_WIDTH = 512
C_GROUP = 128
POOL_HIST = 15
HIST_ROWS = 16

LANES = 128
VMEM_LIMIT = 56 * 1024 * 1024

INT_MIN = -(2 ** 31)
IDX_SUB = 128
Q_SCALE = (A_HEAD_DIM ** -0.5) * float(np.log2(np.e))
BOUND_MARGIN = 1.02
BOUND_SAFE = 60.0

_SRC_SIZES = (("a_q", 1024), ("a_k", 256), ("a_v", 256), ("a_qi", 1024), ("a_ki", 64), ("a_wi", 16),
              ("b_q", 256), ("b_k", 256), ("b_v", 512), ("b_a", 16), ("b_g", 512), ("c_u", 512))
_PACK_ORDER = ("a_q", "a_qi", "b_v", "b_g", "c_u", "a_k", "a_v", "b_q", "b_k", "a_ki", "a_wi", "b_a")


def _build_layout():
    src, off = {}, 0
    for name, w in _SRC_SIZES:
        src[name] = (off, w)
        off += w
    lay, poff = {}, 0
    for name in _PACK_ORDER:
        so, w = src[name]
        pw = -(-w // LANES) * LANES
        lay[name] = (so, w, poff, pw)
        poff += pw
    return lay, poff


_LAYOUT, PACKED_WIDTH = _build_layout()


def _seg(name):
    _, _, poff, pw = _LAYOUT[name]
    return poff, pw


def _dot(a, b):
    return jnp.dot(a, b, preferred_element_type=F32)


def _dot_nt(a, b):
    return lax.dot_general(a, b, (((1,), (1,)), ((), ())), preferred_element_type=F32)


def _dot_tn(a, b):
    return lax.dot_general(a, b, (((0,), (0,)), ((), ())), preferred_element_type=F32)


def _cparams(sem):
    return pltpu.CompilerParams(dimension_semantics=sem, vmem_limit_bytes=VMEM_LIMIT)


def _norm_matmul_kernel(x_ref, g_ref, w_ref, o_ref, xn_ref):
    @pl.when(pl.program_id(1) == 0)
    def _():
        x = x_ref[...]
        ms = jnp.mean(x * x, axis=-1, keepdims=True)
        xn_ref[...] = (x * lax.rsqrt(ms + EPS) * g_ref[...]).astype(BF16)

    o_ref[...] = _dot(xn_ref[...], w_ref[...])


def _norm_matmul(x, g, w, bm, bn):
    M, D = x.shape
    N = w.shape[1]
    return pl.pallas_call(
        _norm_matmul_kernel,
        out_shape=jax.ShapeDtypeStruct((M, N), F32),
        grid=(M // bm, N // bn),
        in_specs=[pl.BlockSpec((bm, D), lambda i, j: (i, 0)),
                  pl.BlockSpec((1, D), lambda i, j: (0, 0)),
                  pl.BlockSpec((D, bn), lambda i, j: (0, j))],
        out_specs=pl.BlockSpec((bm, bn), lambda i, j: (i, j)),
        scratch_shapes=[pltpu.VMEM((bm, D), BF16)],
        compiler_params=_cparams(("arbitrary", "arbitrary")),
        name="norm_matmul",
    )(x, g.reshape(1, D), w)


def _post_kernel(p_ref, cosa_ref, sina_ref, cosb_ref, sinlo_ref, sinhi_ref, gq_ref, gk_ref, gki_ref,
                 walpha_ref, balpha_ref, poolw_ref, pscale_ref, hist0_ref,
                 q_ref, qi_ref, k32_ref, kb_ref, vb_ref, ki32_ref, kieo_ref, wt_ref,
                 bq_ref, bk_ref, bv_ref, lg_ref, gate_ref, oc_ref, ksq_ref,
                 uext_ref, hist_ref, *, tt, pos0):
    i = pl.program_id(1)

    def seg(name):
        off, w = _seg(name)
        return p_ref[:, off:off + w]

    cosa, sina = cosa_ref[...], sina_ref[...]

    def head_norm_rope(a, g):
        r = lax.rsqrt(jnp.mean(a * a, axis=-1, keepdims=True) + EPS)
        y = a * r * g
        return y * cosa + pltpu.roll(y, 64, 1) * sina

    aq = seg("a_q")
    gq = gq_ref[...]
    for h in range(A_HEADS):
        sl = slice(h * LANES, (h + 1) * LANES)
        q_ref[:, sl] = (head_norm_rope(aq[:, sl], gq) * Q_SCALE).astype(BF16)
    ak = seg("a_k")
    gk = gk_ref[...]
    ksq = jnp.zeros((tt, 1), F32)
    for h in range(A_KV_HEADS):
        sl = slice(h * LANES, (h + 1) * LANES)
        kk = head_norm_rope(ak[:, sl], gk)
        k32_ref[:, sl] = kk
        kb_ref[:, sl] = kk.astype(BF16)
        ksq = jnp.maximum(ksq, jnp.sum(kk * kk, axis=-1, keepdims=True))
    ksq_ref[...] = jnp.broadcast_to(jnp.max(ksq, axis=0, keepdims=True), ksq_ref.shape)
    vb_ref[...] = seg("a_v").astype(BF16)

    cosb, sinlo, sinhi = cosb_ref[...], sinlo_ref[...], sinhi_ref[...]

    def rope64(y):
        return y * cosb + pltpu.roll(y, 96, 1) * sinlo + pltpu.roll(y, 32, 1) * sinhi

    aqi = seg("a_qi")
    for gidx in range(IDX_HEADS // 2):
        sl = slice(gidx * LANES, (gidx + 1) * LANES)
        qi_ref[:, sl] = rope64(aqi[:, sl]).astype(BF16)
    aki = seg("a_ki")
    rki = lax.rsqrt(jnp.sum(aki * aki, axis=-1, keepdims=True) * (1.0 / IDX_DIM) + EPS)
    ki = rope64(aki * rki * gki_ref[...])
    ki32_ref[...] = ki[:, :IDX_DIM]
    kieo_ref[:, 0:LANES] = ki.astype(BF16)
    kieo_ref[:, LANES:2 * LANES] = pltpu.roll(ki, 64, 1).astype(BF16)

    awi = seg("a_wi") * IDX_W_SCALE
    if tt % LANES == 0:
        for c in range(tt // LANES):
            wt_ref[:, c * LANES:(c + 1) * LANES] = awi[c * LANES:(c + 1) * LANES, :].T[:IDX_HEADS, :]
    else:
        padded = jnp.concatenate([awi, jnp.zeros((LANES - tt, LANES), F32)], axis=0)
        wt_ref[...] = padded.T[:IDX_HEADS, :tt]

    bq_ref[...] = seg("b_q") * (B_KEY_DIM ** -0.5)
    bk_ref[...] = seg("b_k")
    bv_ref[...] = seg("b_v").astype(BF16)
    z = _dot(seg("b_a").astype(BF16), walpha_ref[...]) + balpha_ref[...]
    lg_ref[...] = (jnp.minimum(z, 0.0) - jnp.log1p(jnp.exp(-jnp.abs(z)))) * (1.0 / B_GATE_TEMP)
    bg = seg("b_g")
    gate_ref[...] = bg * (1.0 / (1.0 + jnp.exp(-bg)))

    @pl.when(i == 0)
    def _():
        hist_ref[...] = hist0_ref[...]

    u = seg("c_u")
    uext_ref[0:HIST_ROWS, :] = hist_ref[...]
    uext_ref[HIST_ROWS:HIST_ROWS + tt, :] = u
    hist_ref[...] = uext_ref[tt:tt + HIST_ROWS, :]
    pos = pos0 + i * tt + lax.broadcasted_iota(jnp.int32, (tt, 1), 0)
    for gi, w in enumerate(C_WINDOWS):
        sl = slice(gi * C_GROUP, (gi + 1) * C_GROUP)
        s = uext_ref[:, sl]
        shift = 1
        while shift < w:
            s = s + pltpu.roll(s, shift, 0)
            shift *= 2
        cnt = jnp.minimum(w, pos + 1).astype(F32)
        d = s[HIST_ROWS:, :] / cnt - u[:, sl]
        y = _dot(d.astype(BF16), poolw_ref[gi]) * pscale_ref[:, sl]
        oc_ref[:, sl] = y.astype(BF16)


def _post(P, tabs, gq, gk, gki, w_alpha, b_alpha, pool_w, pool_scale, hist0, tt, pos0):
    B, T, W = P.shape
    nT = T // tt
    tok = lambda c: pl.BlockSpec((None, tt, c), lambda b, i: (b, i, 0))
    tab = pl.BlockSpec((tt, LANES), lambda b, i: (i, 0))
    full = lambda shape: pl.BlockSpec(shape, lambda b, i: (0,) * len(shape))
    out_shape = [
        jax.ShapeDtypeStruct((B, T, 1024), BF16),
        jax.ShapeDtypeStruct((B, T, 1024), BF16),
        jax.ShapeDtypeStruct((B, T, 256), F32),
        jax.ShapeDtypeStruct((B, T, 256), BF16),
        jax.ShapeDtypeStruct((B, T, 256), BF16),
        jax.ShapeDtypeStruct((B, T, IDX_DIM), F32),
        jax.ShapeDtypeStruct((B, T, 256), BF16),
        jax.ShapeDtypeStruct((B, IDX_HEADS, T), F32),
        jax.ShapeDtypeStruct((B, T, 256), F32),
        jax.ShapeDtypeStruct((B, T, 256), F32),
        jax.ShapeDtypeStruct((B, T, 512), BF16),
        jax.ShapeDtypeStruct((B, T, 256), F32),
        jax.ShapeDtypeStruct((B, T, 512), F32),
        jax.ShapeDtypeStruct((B, T, 512), BF16),
        jax.ShapeDtypeStruct((B, nT, 8, LANES), F32),
    ]
    out_specs = [tok(1024), tok(1024), tok(256), tok(256), tok(256), tok(IDX_DIM), tok(256),
                 pl.BlockSpec((None, IDX_HEADS, tt), lambda b, i: (b, 0, i)),
                 tok(256), tok(256), tok(512), tok(256), tok(512), tok(512),
                 pl.BlockSpec((None, None, 8, LANES), lambda b, i: (b, i, 0, 0))]
    return pl.pallas_call(
        functools.partial(_post_kernel, tt=tt, pos0=pos0),
        out_shape=out_shape,
        grid=(B, nT),
        in_specs=[tok(W), tab, tab, tab, tab, tab,
                  full((1, LANES)), full((1, LANES)), full((1, LANES)),
                  full((LANES, 256)), full((1, 256)),
                  full((len(C_WINDOWS), C_GROUP, C_GROUP)), full((1, C_WIDTH)),
                  pl.BlockSpec((None, HIST_ROWS, C_WIDTH), lambda b, i: (b, 0, 0))],
        out_specs=out_specs,
        scratch_shapes=[pltpu.VMEM((tt + HIST_ROWS, C_WIDTH), F32), pltpu.VMEM((HIST_ROWS, C_WIDTH), F32)],
        compiler_params=_cparams(("arbitrary", "arbitrary")),
        name="post",
    )(P, *tabs, gq, gk, gki, w_alpha, b_alpha, pool_w, pool_scale, hist0)


def _attn_kernel(*refs, tq, kb_size, causal, n_keys, topk, has_bound):
    q_ref, qi_ref, wt_ref, k_ref, v_ref, ki_ref = refs[:6]
    ksq_ref = refs[6] if has_bound else None
    o_ref, key_ref, hi_ref, lo_ref, acc_ref, m_ref, l_ref = refs[-7:]
    i = pl.program_id(1)
    n_slabs = IDX_HEADS // 2
    if causal:
        nkb = ((i + 1) * tq + kb_size - 1) // kb_size
        t_abs = i * tq + lax.broadcasted_iota(jnp.int32, (1, tq), 1)
        limit = (t_abs // CHUNK + 1) * CHUNK
    else:
        nkb = -(-n_keys // kb_size)
        limit = jnp.full((1, tq), n_keys, jnp.int32)

    q2 = jnp.concatenate([qi_ref[:, g * LANES:(g + 1) * LANES] for g in range(n_slabs)], axis=0)

    def idx_body(kb, carry):
        for sb in range(kb_size // IDX_SUB):
            off = pl.multiple_of(kb * kb_size + sb * IDX_SUB, IDX_SUB)
            ye = _dot_nt(ki_ref[pl.ds(off, IDX_SUB), 0:LANES], q2)
            yo = _dot_nt(ki_ref[pl.ds(off, IDX_SUB), LANES:2 * LANES], q2)
            score = jnp.zeros((IDX_SUB, tq), F32)
            for g in range(n_slabs):
                score = score + jnp.maximum(ye[:, g * tq:(g + 1) * tq], 0.0) * wt_ref[2 * g:2 * g + 1, :]
                score = score + jnp.maximum(yo[:, g * tq:(g + 1) * tq], 0.0) * wt_ref[2 * g + 1:2 * g + 2, :]
            s_idx = off + lax.broadcasted_iota(jnp.int32, (IDX_SUB, 1), 0)
            bits = pltpu.bitcast(score, jnp.int32)
            key = bits ^ ((bits >> 31) & jnp.int32(0x7FFFFFFF))
            key = jnp.where(s_idx < limit, key, jnp.int32(INT_MIN))
            key_ref[pl.ds(off, IDX_SUB), :] = key
            hi_ref[pl.ds(off, IDX_SUB), :] = (key >> 16).astype(jnp.int16)
            lo_ref[pl.ds(off, IDX_SUB), :] = ((key & jnp.int32(0xFFFF)) - 32768).astype(jnp.int16)
        return carry

    lax.fori_loop(0, nkb, idx_body, 0)

    def bcast16(x):
        return jnp.broadcast_to(x, (16, tq)).astype(jnp.int16)

    def blocks16(ref, kb):
        off = pl.multiple_of(kb * kb_size, kb_size)
        return ref[pl.ds(off, kb_size), :].reshape(kb_size // 16, 16, tq)

    def total(acc):
        return jnp.sum(acc.astype(jnp.int32), axis=0, keepdims=True)

    one16, zero16 = jnp.int16(1), jnp.int16(0)

    def tree_sum(x):
        parts = [x[j] for j in range(x.shape[0])]
        while len(parts) > 1:
            parts = [parts[j] + parts[j + 1] for j in range(0, len(parts) - 1, 2)] + parts[len(parts) & ~1:]
        return parts[0]

    def search16(ref, target):
        def bit_body(bi, ut):
            cand = ut | lax.shift_left(jnp.int32(1), 15 - bi)
            c16 = bcast16(cand - 32768)

            def cnt_body(kb, acc):
                return acc + tree_sum(jnp.where(blocks16(ref, kb) >= c16[None], one16, zero16))

            cnt = total(lax.fori_loop(0, nkb, cnt_body, jnp.zeros((16, tq), jnp.int16)))
            return jnp.where(cnt >= target, cand, ut)

        return lax.fori_loop(0, 16, bit_body, jnp.zeros((1, tq), jnp.int32))

    p_hi = search16(hi_ref, topk) - 32768
    p16 = bcast16(p_hi)

    def tie_body(kb, acc):
        off = pl.multiple_of(kb * kb_size, kb_size)
        hi = blocks16(hi_ref, kb)
        lo = blocks16(lo_ref, kb)
        lo_ref[pl.ds(off, kb_size), :] = jnp.where(hi == p16[None], lo, jnp.int16(-32768)).reshape(kb_size, tq)
        return acc + tree_sum(jnp.where(hi > p16[None], one16, zero16))

    n_gt = total(lax.fori_loop(0, nkb, tie_body, jnp.zeros((16, tq), jnp.int16)))
    p_lo = search16(lo_ref, topk - n_gt)
    tau = jnp.maximum(lax.shift_left(p_hi, 16) | p_lo, jnp.int32(INT_MIN + 1))

    def count32(pred):
        def body(kb, acc):
            off = pl.multiple_of(kb * kb_size, kb_size)
            s_idx = off + lax.broadcasted_iota(jnp.int32, (kb_size, 1), 0)
            hit = jnp.where(pred(key_ref[pl.ds(off, kb_size), :], s_idx), 1, 0)
            return acc + jnp.sum(hit.reshape(kb_size // 8, 8, tq), axis=0)

        return jnp.sum(lax.fori_loop(0, nkb, body, jnp.zeros((8, tq), jnp.int32)), axis=0, keepdims=True)

    n_ge = count32(lambda k, s: k >= tau)

    @pl.when(jnp.max(n_ge) > topk)
    def _():
        quota = topk - count32(lambda k, s: k > tau)
        idx_bits = int(key_ref.shape[0]).bit_length()

        def bit_body(bi, cut):
            cand = cut | lax.shift_left(jnp.int32(1), idx_bits - 1 - bi)
            kept = count32(lambda k, s: jnp.where(k == tau, s, cand) < cand)
            return jnp.where(kept <= quota, cand, cut)

        cut = lax.fori_loop(0, idx_bits, bit_body, jnp.zeros((1, tq), jnp.int32))

        def demote(kb, carry):
            off = pl.multiple_of(kb * kb_size, kb_size)
            s_idx = off + lax.broadcasted_iota(jnp.int32, (kb_size, 1), 0)
            blk = key_ref[pl.ds(off, kb_size), :]
            key_ref[pl.ds(off, kb_size), :] = jnp.where(jnp.where(blk == tau, s_idx, -1) >= cut, tau - 1, blk)
            return carry

        lax.fori_loop(0, nkb, demote, 0)

    hpg = A_HEADS // A_KV_HEADS
    qgs = [jnp.concatenate([q_ref[:, (g * hpg + h) * LANES:(g * hpg + h + 1) * LANES] for h in range(hpg)], axis=0)
           for g in range(A_KV_HEADS)]
    l_ref[...] = jnp.zeros(l_ref.shape, F32)
    acc_ref[...] = jnp.zeros(acc_ref.shape, F32)

    def masked_scores(kb, g, sel, fill, shift):
        off = pl.multiple_of(kb * kb_size, kb_size)
        s = _dot_nt(k_ref[pl.ds(off, kb_size), g * LANES:(g + 1) * LANES], qgs[g])
        return jnp.concatenate([jnp.where(sel, shift(s[:, h * tq:(h + 1) * tq], h), fill) for h in range(hpg)], axis=1)

    def add_pv(kb, g, p):
        off = pl.multiple_of(kb * kb_size, kb_size)
        return _dot_tn(v_ref[pl.ds(off, kb_size), g * LANES:(g + 1) * LANES], p.astype(BF16))

    def selected(kb):
        off = pl.multiple_of(kb * kb_size, kb_size)
        return key_ref[pl.ds(off, kb_size), :] >= tau

    def online():
        m_ref[...] = jnp.full(m_ref.shape, -1e30, F32)

        def body(kb, carry):
            sel = selected(kb)
            for g in range(A_KV_HEADS):
                s = masked_scores(kb, g, sel, -jnp.inf, lambda x, h: x)
                m_old = m_ref[g]
                m_new = jnp.maximum(m_old, jnp.max(s, axis=0, keepdims=True))
                alpha = jnp.exp2(m_old - m_new)
                p = jnp.exp2(s - m_new)
                l_ref[g] = alpha * l_ref[g] + jnp.sum(p, axis=0, keepdims=True)
                acc_ref[g] = acc_ref[g] * alpha + add_pv(kb, g, p)
                m_ref[g] = m_new
            return carry

        lax.fori_loop(0, nkb, body, 0)

    def bounded(bounds):
        def body(kb, carry):
            sel = selected(kb)
            for g in range(A_KV_HEADS):
                b = bounds[g]
                p = masked_scores(kb, g, sel, 0.0, lambda x, h: jnp.exp2(x - b[:, h * tq:(h + 1) * tq]))
                l_ref[g] = l_ref[g] + jnp.sum(p, axis=0, keepdims=True)
                acc_ref[g] = acc_ref[g] + add_pv(kb, g, p)
            return carry

        lax.fori_loop(0, nkb, body, 0)

    if ksq_ref is None:
        online()
    else:
        kmax = jnp.sqrt(jnp.max(ksq_ref[...]))
        ones = jnp.ones((8, LANES), BF16)
        bounds = [jnp.sqrt(_dot_nt(ones, qg * qg)[0:1, :]) * (kmax * BOUND_MARGIN) for qg in qgs]
        safe = jnp.max(jnp.maximum(bounds[0], bounds[1])) <= BOUND_SAFE

        @pl.when(safe)
        def _():
            bounded(bounds)

        @pl.when(jnp.logical_not(safe))
        def _():
            online()

    for g in range(A_KV_HEADS):
        o = acc_ref[g] / l_ref[g]
        for h in range(hpg):
            hh = g * hpg + h
            o_ref[:, hh * LANES:(hh + 1) * LANES] = o[:, h * tq:(h + 1) * tq].T.astype(o_ref.dtype)


def _attn(q, qi, wt, kb, vb, kieo, ksq, tq, kb_size, causal, n_keys, topk):
    B, T, _ = q.shape
    S = kb.shape[1]
    hpg = A_HEADS // A_KV_HEADS
    kern = functools.partial(_attn_kernel, tq=tq, kb_size=kb_size, causal=causal, n_keys=n_keys, topk=topk,
                             has_bound=ksq is not None)
    in_specs = [pl.BlockSpec((None, tq, 1024), lambda b, i: (b, i, 0)),
                pl.BlockSpec((None, tq, 1024), lambda b, i: (b, i, 0)),
                pl.BlockSpec((None, IDX_HEADS, tq), lambda b, i: (b, 0, i)),
                pl.BlockSpec((None, S, 256), lambda b, i: (b, 0, 0)),
                pl.BlockSpec((None, S, 256), lambda b, i: (b, 0, 0)),
                pl.BlockSpec((None, S, 256), lambda b, i: (b, 0, 0))]
    args = [q, qi, wt, kb, vb, kieo]
    if ksq is not None:
        in_specs.append(pl.BlockSpec((None,) + ksq.shape[1:], lambda b, i: (b, 0, 0, 0)))
        args.append(ksq)
    return pl.pallas_call(
        kern,
        out_shape=jax.ShapeDtypeStruct((B, T, 1024), BF16),
        grid=(B, T // tq),
        in_specs=in_specs,
        out_specs=pl.BlockSpec((None, tq, 1024), lambda b, i: (b, i, 0)),
        scratch_shapes=[pltpu.VMEM((S, tq), jnp.int32),
                        pltpu.VMEM((S, tq), jnp.int16),
                        pltpu.VMEM((S, tq), jnp.int16),
                        pltpu.VMEM((A_KV_HEADS, A_HEAD_DIM, hpg * tq), F32),
                        pltpu.VMEM((A_KV_HEADS, 1, hpg * tq), F32),
                        pltpu.VMEM((A_KV_HEADS, 1, hpg * tq), F32)],
        compiler_params=_cparams(("arbitrary", "arbitrary")),
        name="attn",
    )(*args)


def _gla_consts(C):
    nl = int(np.log2(C))
    M = np.zeros((nl + 2, C, C), np.float32)
    masks = np.zeros((nl + 1, C, C), np.float32)
    for j in range(nl):
        m = 1 << j
        for t in range(C):
            blk = (t // (2 * m)) * 2 * m
            mid = blk + m
            if t >= mid:
                M[j, t, mid:t + 1] = 1
                masks[j, t, blk:mid] = 1
            else:
                M[j, t, t + 1:mid] = 1
    for t in range(C):
        M[nl, t, :t + 1] = 1
        M[nl + 1, t, t + 1:] = 1
    masks[nl] = np.eye(C)
    return M.reshape((nl + 2) * C, C), np.concatenate([masks, masks], axis=1)


def _gla_kernel(q_ref, k_ref, v_ref, lg_ref, gate_ref, mcat_ref, masks_ref, gn_ref, s0_ref,
                o_ref, sout_ref, s_ref, *, tt, C):
    i = pl.program_id(1)
    nl = int(np.log2(C))

    @pl.when(i == 0)
    def _():
        s_ref[...] = s0_ref[...]

    lane = lax.broadcasted_iota(jnp.int32, (1, LANES), 1)
    low = lane < B_KEY_DIM
    row = lax.broadcasted_iota(jnp.int32, (C, 1), 0)
    rr = lax.broadcasted_iota(jnp.int32, (LANES, 2 * B_VAL_DIM), 0)
    cc = lax.broadcasted_iota(jnp.int32, (LANES, 2 * B_VAL_DIM), 1)
    blockdiag = (rr < B_KEY_DIM) == (cc < B_VAL_DIM)
    gn = gn_ref[...]

    def stack2(x):
        return jnp.concatenate([jnp.where(low, x, 0.0), jnp.where(low, 0.0, x)], axis=0).astype(BF16)

    def chunk_body(c, carry):
        r0 = pl.multiple_of(c * C, C)
        qc = q_ref[pl.ds(r0, C), :]
        kc = k_ref[pl.ds(r0, C), :]
        vc = v_ref[pl.ds(r0, C), :]
        lgc = lg_ref[pl.ds(r0, C), :]
        hi = lgc.astype(BF16)
        r1 = lgc - hi.astype(F32)
        mid = r1.astype(BF16)
        lo = (r1 - mid.astype(F32)).astype(BF16)
        mcat = mcat_ref[...]
        E = jnp.exp(_dot(mcat, hi) + _dot(mcat, mid) + _dot(mcat, lo))
        e_b = E[nl * C:(nl + 1) * C, :]
        e_k = E[(nl + 1) * C:(nl + 2) * C, :]
        qt = qc * e_b
        kt = kc * e_k
        e_last = e_b[C - 1:C, :]
        for g in range(B_HEADS // 2):
            sl = slice(g * LANES, (g + 1) * LANES)
            att = _dot_nt(stack2(qc[:, sl]), kc[:, sl].astype(BF16)) * masks_ref[nl]
            for j in range(nl):
                right = ((row >> j) & 1) == 1
                x = jnp.where(right, qc[:, sl], kc[:, sl]) * E[j * C:(j + 1) * C, sl]
                att = att + _dot_nt(stack2(x), x.astype(BF16)) * masks_ref[j]
            att = att.astype(BF16)
            vg = vc[:, g * 2 * B_VAL_DIM:(g + 1) * 2 * B_VAL_DIM]
            sg = s_ref[g]
            o_intra = jnp.concatenate([_dot(att[0:C], vg[:, 0:B_VAL_DIM]),
                                       _dot(att[C:2 * C], vg[:, B_VAL_DIM:2 * B_VAL_DIM])], axis=1)
            o = o_intra + _dot(qt[:, sl].astype(BF16), sg.astype(BF16))
            upd = _dot_tn(kt[:, sl].astype(BF16), vg)
            dec = jnp.broadcast_to(e_last[:, sl], (LANES, LANES)).T
            dec = jnp.concatenate([dec, dec], axis=1)
            s_ref[g] = dec * sg + jnp.where(blockdiag, upd, 0.0)
            for hh in range(2):
                h = 2 * g + hh
                oh = o[:, hh * B_VAL_DIM:(hh + 1) * B_VAL_DIM]
                r = lax.rsqrt(jnp.mean(oh * oh, axis=-1, keepdims=True) + EPS)
                gsl = slice(h * B_VAL_DIM, (h + 1) * B_VAL_DIM)
                o_ref[pl.ds(r0, C), gsl] = (oh * r * gn * gate_ref[pl.ds(r0, C), gsl]).astype(o_ref.dtype)
        return carry

    lax.fori_loop(0, tt // C, chunk_body, 0)

    @pl.when(i == pl.num_programs(1) - 1)
    def _():
        sout_ref[...] = s_ref[...]


def _gla(bq, bk, bv, lg, gate, gn, s0, tt, C):
    B, T, _ = bq.shape
    mcat, masks = _gla_consts(C)
    nl = int(np.log2(C))
    tok = lambda c: pl.BlockSpec((None, tt, c), lambda b, i: (b, i, 0))
    full = lambda shape: pl.BlockSpec(shape, lambda b, i: (0,) * len(shape))
    st = pl.BlockSpec((None, 2, LANES, 2 * B_VAL_DIM), lambda b, i: (b, 0, 0, 0))
    return pl.pallas_call(
        functools.partial(_gla_kernel, tt=tt, C=C),
        out_shape=[jax.ShapeDtypeStruct((B, T, 512), BF16),
                   jax.ShapeDtypeStruct((B, 2, LANES, 2 * B_VAL_DIM), F32)],
        grid=(B, T // tt),
        in_specs=[tok(256), tok(256), tok(512), tok(256), tok(512),
                  full(((nl + 2) * C, C)), full((nl + 1, 2 * C, C)), full((1, B_VAL_DIM)), st],
        out_specs=[tok(512), st],
        scratch_shapes=[pltpu.VMEM((2, LANES, 2 * B_VAL_DIM), F32)],
        compiler_params=_cparams(("arbitrary", "arbitrary")),
        name="gla",
    )(bq, bk, bv, lg, gate, jnp.asarray(mcat, BF16), jnp.asarray(masks, F32), gn, s0)


def _out_kernel(x_ref, oa_ref, ob_ref, oc_ref, wo_ref, g2_ref, h_ref, m_ref):
    acc = _dot(oa_ref[...], wo_ref[0:1024, :])
    acc = acc + _dot(ob_ref[...], wo_ref[1024:1536, :])
    acc = acc + _dot(oc_ref[...], wo_ref[1536:2048, :])
    h = x_ref[...] + acc
    h_ref[...] = h
    ms = jnp.mean(h * h, axis=-1, keepdims=True)
    m_ref[...] = (h * lax.rsqrt(ms + EPS) * g2_ref[...]).astype(BF16)


def _out_proj(x, oa, ob, oc, wo, g2, bm):
    M, D = x.shape
    row = lambda c: pl.BlockSpec((bm, c), lambda i: (i, 0))
    return pl.pallas_call(
        _out_kernel,
        out_shape=[jax.ShapeDtypeStruct((M, D), F32), jax.ShapeDtypeStruct((M, D), BF16)],
        grid=(M // bm,),
        in_specs=[row(D), row(1024), row(512), row(512),
                  pl.BlockSpec((D, D), lambda i: (0, 0)), pl.BlockSpec((1, D), lambda i: (0, 0))],
        out_specs=[row(D), row(D)],
        compiler_params=_cparams(("arbitrary",)),
        name="out_proj",
    )(x, oa, ob, oc, wo, g2.reshape(1, D))


def _ffn_kernel(h_ref, m_ref, wg_ref, wu_ref, wd_ref, y_ref):
    @pl.when(pl.program_id(1) == 0)
    def _():
        y_ref[...] = h_ref[...]

    m = m_ref[...]
    gate = _dot(m, wg_ref[...])
    up = _dot(m, wu_ref[...])
    a = (gate * (1.0 / (1.0 + jnp.exp(-gate))) * up).astype(BF16)
    y_ref[...] += _dot(a, wd_ref[...])


def _ffn(h, m, wg, wu, wd, bm, bf):
    M, D = h.shape
    Fh = wg.shape[1]
    return pl.pallas_call(
        _ffn_kernel,
        out_shape=jax.ShapeDtypeStruct((M, D), F32),
        grid=(M // bm, Fh // bf),
        in_specs=[pl.BlockSpec((bm, D), lambda i, j: (i, 0)),
                  pl.BlockSpec((bm, D), lambda i, j: (i, 0)),
                  pl.BlockSpec((D, bf), lambda i, j: (0, j)),
                  pl.BlockSpec((D, bf), lambda i, j: (0, j)),
                  pl.BlockSpec((bf, D), lambda i, j: (j, 0))],
        out_specs=pl.BlockSpec((bm, D), lambda i, j: (i, 0)),
        compiler_params=_cparams(("arbitrary", "arbitrary")),
        name="ffn",
    )(h, m, wg, wu, wd)


def _rope_tables(pos0, T):
    pos = (pos0 + jnp.arange(T, dtype=jnp.int32)).astype(F32)

    def cs(half):
        inv = ROPE_THETA ** (-jnp.arange(half, dtype=F32) / half)
        ang = pos[:, None] * inv[None, :]
        return jnp.cos(ang), jnp.sin(ang)

    c64, s64 = cs(A_HEAD_DIM // 2)
    c32, s32 = cs(IDX_DIM // 2)
    z32 = jnp.zeros_like(s32)
    return (jnp.concatenate([c64, c64], axis=1), jnp.concatenate([-s64, s64], axis=1),
            jnp.concatenate([c32, c32, c32, c32], axis=1),
            jnp.concatenate([-s32, z32, -s32, z32], axis=1),
            jnp.concatenate([z32, s32, z32, s32], axis=1))


def _pack_w_in(w_in):
    cols = []
    for name in _PACK_ORDER:
        so, w, _, pw = _LAYOUT[name]
        piece = w_in[:, :, so:so + w]
        if pw > w:
            piece = jnp.pad(piece, ((0, 0), (0, 0), (0, pw - w)))
        cols.append(piece)
    return jnp.concatenate(cols, axis=-1).astype(BF16)


def _state_to_groups(s):
    B = s.shape[0]
    s = s.reshape(B, 2, 2, B_KEY_DIM, B_VAL_DIM)
    z = jnp.zeros_like(s[:, :, 0])
    top = jnp.concatenate([s[:, :, 0], z], axis=-1)
    bot = jnp.concatenate([z, s[:, :, 1]], axis=-1)
    return jnp.concatenate([top, bot], axis=-2)


def _groups_to_state(sg):
    B = sg.shape[0]
    h0 = sg[:, :, :B_KEY_DIM, :B_VAL_DIM]
    h1 = sg[:, :, B_KEY_DIM:, B_VAL_DIM:]
    return jnp.stack([h0, h1], axis=2).reshape(B, B_HEADS, B_KEY_DIM, B_VAL_DIM)


def _layer(x, pos0, cache, prm):
    (g1, w_in_p, gq, gk, gki, w_alpha, b_alpha, g_gla, pool_w, pool_scale, w_o, g2, w_gate, w_up, w_down) = prm
    B, T, D = x.shape
    M = B * T
    x2 = x.reshape(M, D)
    bm = min(512, M)
    P = _norm_matmul(x2, g1, w_in_p, bm, PACKED_WIDTH // 3).reshape(B, T, PACKED_WIDTH)

    tt = min(512, T)
    tabs = _rope_tables(pos0, T)
    gki_p = jnp.pad(gki, (0, LANES - IDX_DIM)).reshape(1, LANES)
    w_alpha_p = jnp.pad(w_alpha, ((0, LANES - B_GATE_RANK), (0, 0))).astype(BF16)
    if cache is None:
        hist0 = jnp.zeros((B, HIST_ROWS, C_WIDTH), F32)
    else:
        hist0 = jnp.pad(cache[4], ((0, 0), (HIST_ROWS - POOL_HIST, 0), (0, 0)))
    (q, qi, k32, kb, vb, ki32, kieo, wt, bq, bk, bv, lg, gate, oc, ksq) = _post(
        P, tabs, gq.reshape(1, LANES), gk.reshape(1, LANES), gki_p, w_alpha_p, b_alpha.reshape(1, -1),
        pool_w.astype(BF16), pool_scale.reshape(1, -1), hist0, tt, pos0)

    v_off, _ = _seg("a_v")
    v32 = P[:, :, v_off:v_off + 256]
    u_off, _ = _seg("c_u")
    pool_state = P[:, T - POOL_HIST:, u_off:u_off + C_WIDTH]

    if cache is None:
        tq = LANES
        oa = _attn(q, qi, wt, kb, vb, kieo, ksq, tq, 512, True, T, min(TOPK_MAX, T // 4))
        s0 = jnp.zeros((B, 2, LANES, 2 * B_VAL_DIM), F32)
        ob, sg = _gla(bq, bk, bv, lg, gate, g_gla.reshape(1, -1), s0, tt, CHUNK)
    else:
        ck, cv, cki, c_gla, _ = cache
        past = ck.shape[1]
        L = past + T
        kb_size = 384
        S = -(-L // kb_size) * kb_size
        padk = lambda a: jnp.pad(a, ((0, 0), (0, S - L), (0, 0)))
        k_all = padk(jnp.concatenate([ck.reshape(B, past, 256).astype(BF16), kb], axis=1))
        v_all = padk(jnp.concatenate([cv.reshape(B, past, 256).astype(BF16), vb], axis=1))
        zc = jnp.zeros((B, past, IDX_DIM), BF16)
        cki_b = cki.astype(BF16)
        ki_all = padk(jnp.concatenate([jnp.concatenate([cki_b, zc, zc, cki_b], axis=-1), kieo], axis=1))
        tq = LANES
        padq = lambda a: jnp.pad(a, ((0, 0), (0, tq - T), (0, 0)))
        wt_p = jnp.pad(wt, ((0, 0), (0, 0), (0, tq - T)))
        oa = _attn(padq(q), padq(qi), wt_p, k_all, v_all, ki_all, None, tq, kb_size, False, L,
                   min(TOPK_MAX, L // 4))[:, :T]
        ob, sg = _gla(bq, bk, bv, lg, gate, g_gla.reshape(1, -1), _state_to_groups(c_gla), tt, T)

    h, m = _out_proj(x2, oa.reshape(M, -1), ob.reshape(M, -1), oc.reshape(M, -1), w_o, g2, bm)
    y = _ffn(h, m, w_gate, w_up, w_down, bm, 512)
    state = (k32.reshape(B, T, A_KV_HEADS, A_HEAD_DIM), v32.reshape(B, T, A_KV_HEADS, A_HEAD_DIM),
             ki32, _groups_to_state(sg), pool_state)
    return y.reshape(B, T, D), state


def kernel(x_prompt, x_sample, cache_k, cache_v, cache_kidx, state_gla, state_pool,
           norm1, w_in, q_norm, k_norm, kidx_norm, w_alpha, b_alpha, gla_norm, pool_w, pool_scale,
           w_o, norm2, w_gate, w_up, w_down):
    depth = w_in.shape[0]
    past = cache_k.shape[2]
    w_in_p = _pack_w_in(w_in)
    w_o_b, w_gate_b, w_up_b, w_down_b = (w.astype(BF16) for w in (w_o, w_gate, w_up, w_down))
    yp, ys = x_prompt, x_sample
    st_p, st_s = [], []
    for l in range(depth):
        prm = (norm1[l], w_in_p[l], q_norm[l], k_norm[l], kidx_norm[l], w_alpha[l], b_alpha[l], gla_norm[l],
               pool_w[l], pool_scale[l], w_o_b[l], norm2[l], w_gate_b[l], w_up_b[l], w_down_b[l])
        yp, sp = _layer(yp, 0, None, prm)
        ys, ss = _layer(ys, past, (cache_k[l], cache_v[l], cache_kidx[l], state_gla[l], state_pool[l]), prm)
        st_p.append(sp)
        st_s.append(ss)
    stk = lambda sts, i: jnp.stack([s[i] for s in sts], axis=0)
    return (yp, ys,
            stk(st_p, 0), stk(st_p, 1), stk(st_p, 2), stk(st_p, 3), stk(st_p, 4),
            stk(st_s, 0), stk(st_s, 1), stk(st_s, 2), stk(st_s, 3), stk(st_s, 4))
```

```python
import functools

import numpy as np
import jax
import jax.numpy as jnp
from jax import lax
from jax.experimental import pallas as pl
from jax.experimental.pallas import tpu as pltpu

F32 = jnp.float32
BF16 = jnp.bfloat16

EPS = 1e-6
CHUNK = 64
TOPK_MAX = 256
ROPE_THETA = 10000.0
A_HEAD_DIM = 128
A_HEADS = 8
A_KV_HEADS = 2
IDX_HEADS = 16
IDX_DIM = 64
IDX_W_SCALE = (IDX_HEADS ** -0.5) * (IDX_DIM ** -0.5)
B_HEADS = 4
B_KEY_DIM = 64
B_VAL_DIM = 128
B_GATE_RANK = 16
B_GATE_TEMP = 16.0
C_WINDOWS = (2, 4, 8, 16)
C_WIDTH = 512
C_GROUP = 128
POOL_HIST = 15
HIST_ROWS = 16

LANES = 128
VMEM_LIMIT = 56 * 1024 * 1024

INT_MIN = -(2 ** 31)
IDX_SUB = 256
Q_SCALE = (A_HEAD_DIM ** -0.5) * float(np.log2(np.e))
BOUND_MARGIN = 1.02
BOUND_SAFE = 60.0

_SRC_SIZES = (("a_q", 1024), ("a_k", 256), ("a_v", 256), ("a_qi", 1024), ("a_ki", 64), ("a_wi", 16),
              ("b_q", 256), ("b_k", 256), ("b_v", 512), ("b_a", 16), ("b_g", 512), ("c_u", 512))
_PACK_ORDER = ("a_q", "a_qi", "b_v", "b_g", "c_u", "a_k", "a_v", "b_q", "b_k", "a_ki", "a_wi", "b_a")


def _build_layout():
    src, off = {}, 0
    for name, w in _SRC_SIZES:
        src[name] = (off, w)
        off += w
    lay, poff = {}, 0
    for name in _PACK_ORDER:
        so, w = src[name]
        pw = -(-w // LANES) * LANES
        lay[name] = (so, w, poff, pw)
        poff += pw
    return lay, poff


_LAYOUT, PACKED_WIDTH = _build_layout()


def _seg(name):
    _, _, poff, pw = _LAYOUT[name]
    return poff, pw


def _dot(a, b):
    return jnp.dot(a, b, preferred_element_type=F32)


def _dot_nt(a, b):
    return lax.dot_general(a, b, (((1,), (1,)), ((), ())), preferred_element_type=F32)


def _dot_tn(a, b):
    return lax.dot_general(a, b, (((0,), (0,)), ((), ())), preferred_element_type=F32)


def _bit_transpose32(rows):
    a = list(rows)
    j, m = 16, 0x0000FFFF
    while j:
        k = 0
        while k < 32:
            t = (a[k] ^ lax.shift_right_logical(a[k + j], jnp.int32(j))) & jnp.int32(m)
            a[k] = a[k] ^ t
            a[k + j] = a[k + j] ^ lax.shift_left(t, jnp.int32(j))
            k = (k + j + 1) & ~j
        j >>= 1
        m = (m ^ (m << j)) & 0xFFFFFFFF
    return a


def _cparams(sem):
    return pltpu.CompilerParams(dimension_semantics=sem, vmem_limit_bytes=VMEM_LIMIT)


def _norm_matmul_kernel(x_ref, g_ref, w_ref, o_ref, xn_ref):
    @pl.when(pl.program_id(1) == 0)
    def _():
        x = x_ref[...]
        ms = jnp.mean(x * x, axis=-1, keepdims=True)
        xn_ref[...] = (x * lax.rsqrt(ms + EPS) * g_ref[...]).astype(BF16)

    o_ref[...] = _dot(xn_ref[...], w_ref[...])


def _norm_matmul(x, g, w, bm, bn):
    M, D = x.shape
    N = w.shape[1]
    return pl.pallas_call(
        _norm_matmul_kernel,
        out_shape=jax.ShapeDtypeStruct((M, N), F32),
        grid=(M // bm, N // bn),
        in_specs=[pl.BlockSpec((bm, D), lambda i, j: (i, 0)),
                  pl.BlockSpec((1, D), lambda i, j: (0, 0)),
                  pl.BlockSpec((D, bn), lambda i, j: (0, j))],
        out_specs=pl.BlockSpec((bm, bn), lambda i, j: (i, j)),
        scratch_shapes=[pltpu.VMEM((bm, D), BF16)],
        compiler_params=_cparams(("arbitrary", "arbitrary")),
        name="norm_matmul",
    )(x, g.reshape(1, D), w)


def _post_kernel(p_ref, cosa_ref, sina_ref, cosb_ref, sinlo_ref, sinhi_ref, gq_ref, gk_ref, gki_ref,
                 walpha_ref, balpha_ref, poolw_ref, pscale_ref, hist0_ref,
                 q_ref, qi_ref, k32_ref, kb_ref, vb_ref, ki32_ref, kieo_ref, wt_ref,
                 bq_ref, bk_ref, bv_ref, lg_ref, gate_ref, oc_ref, ksq_ref,
                 uext_ref, hist_ref, *, tt, pos0):
    i = pl.program_id(1)

    def seg(name):
        off, w = _seg(name)
        return p_ref[:, off:off + w]

    cosa, sina = cosa_ref[...], sina_ref[...]

    def head_norm_rope(a, g):
        r = lax.rsqrt(jnp.mean(a * a, axis=-1, keepdims=True) + EPS)
        y = a * r * g
        return y * cosa + pltpu.roll(y, 64, 1) * sina

    aq = seg("a_q")
    gq = gq_ref[...]
    for h in range(A_HEADS):
        sl = slice(h * LANES, (h + 1) * LANES)
        q_ref[:, sl] = (head_norm_rope(aq[:, sl], gq) * Q_SCALE).astype(BF16)
    ak = seg("a_k")
    gk = gk_ref[...]
    ksq = jnp.zeros((tt, 1), F32)
    for h in range(A_KV_HEADS):
        sl = slice(h * LANES, (h + 1) * LANES)
        kk = head_norm_rope(ak[:, sl], gk)
        k32_ref[:, sl] = kk
        kb_ref[:, sl] = kk.astype(BF16)
        ksq = jnp.maximum(ksq, jnp.sum(kk * kk, axis=-1, keepdims=True))
    ksq_ref[...] = jnp.broadcast_to(jnp.max(ksq, axis=0, keepdims=True), ksq_ref.shape)
    vb_ref[...] = seg("a_v").astype(BF16)

    cosb, sinlo, sinhi = cosb_ref[...], sinlo_ref[...], sinhi_ref[...]

    def rope64(y):
        return y * cosb + pltpu.roll(y, 96, 1) * sinlo + pltpu.roll(y, 32, 1) * sinhi

    aqi = seg("a_qi")
    for gidx in range(IDX_HEADS // 2):
        sl = slice(gidx * LANES, (gidx + 1) * LANES)
        qi_ref[:, sl] = rope64(aqi[:, sl]).astype(BF16)
    aki = seg("a_ki")
    rki = lax.rsqrt(jnp.sum(aki * aki, axis=-1, keepdims=True) * (1.0 / IDX_DIM) + EPS)
    ki = rope64(aki * rki * gki_ref[...])
    ki32_ref[...] = ki[:, :IDX_DIM]
    kieo_ref[:, 0:LANES] = ki.astype(BF16)
    kieo_ref[:, LANES:2 * LANES] = pltpu.roll(ki, 64, 1).astype(BF16)

    awi = seg("a_wi") * IDX_W_SCALE
    if tt % LANES == 0:
        for c in range(tt // LANES):
            wt_ref[:, c * LANES:(c + 1) * LANES] = awi[c * LANES:(c + 1) * LANES, :].T[:IDX_HEADS, :]
    else:
        padded = jnp.concatenate([awi, jnp.zeros((LANES - tt, LANES), F32)], axis=0)
        wt_ref[...] = padded.T[:IDX_HEADS, :tt]

    bq_ref[...] = seg("b_q") * (B_KEY_DIM ** -0.5)
    bk_ref[...] = seg("b_k")
    bv_ref[...] = seg("b_v").astype(BF16)
    z = _dot(seg("b_a").astype(BF16), walpha_ref[...]) + balpha_ref[...]
    lg_ref[...] = (jnp.minimum(z, 0.0) - jnp.log1p(jnp.exp(-jnp.abs(z)))) * (1.0 / B_GATE_TEMP)
    bg = seg("b_g")
    gate_ref[...] = bg * (1.0 / (1.0 + jnp.exp(-bg)))

    @pl.when(i == 0)
    def _():
        hist_ref[...] = hist0_ref[...]

    u = seg("c_u")
    uext_ref[0:HIST_ROWS, :] = hist_ref[...]
    uext_ref[HIST_ROWS:HIST_ROWS + tt, :] = u
    hist_ref[...] = uext_ref[tt:tt + HIST_ROWS, :]
    pos = pos0 + i * tt + lax.broadcasted_iota(jnp.int32, (tt, 1), 0)
    for gi, w in enumerate(C_WINDOWS):
        sl = slice(gi * C_GROUP, (gi + 1) * C_GROUP)
        s = uext_ref[:, sl]
        shift = 1
        while shift < w:
            s = s + pltpu.roll(s, shift, 0)
            shift *= 2
        cnt = jnp.minimum(w, pos + 1).astype(F32)
        d = s[HIST_ROWS:, :] / cnt - u[:, sl]
        y = _dot(d.astype(BF16), poolw_ref[gi]) * pscale_ref[:, sl]
        oc_ref[:, sl] = y.astype(BF16)


def _post(P, tabs, gq, gk, gki, w_alpha, b_alpha, pool_w, pool_scale, hist0, tt, pos0):
    B, T, W = P.shape
    nT = T // tt
    tok = lambda c: pl.BlockSpec((None, tt, c), lambda b, i: (b, i, 0))
    tab = pl.BlockSpec((tt, LANES), lambda b, i: (i, 0))
    full = lambda shape: pl.BlockSpec(shape, lambda b, i: (0,) * len(shape))
    out_shape = [
        jax.ShapeDtypeStruct((B, T, 1024), BF16),
        jax.ShapeDtypeStruct((B, T, 1024), BF16),
        jax.ShapeDtypeStruct((B, T, 256), F32),
        jax.ShapeDtypeStruct((B, T, 256), BF16),
        jax.ShapeDtypeStruct((B, T, 256), BF16),
        jax.ShapeDtypeStruct((B, T, IDX_DIM), F32),
        jax.ShapeDtypeStruct((B, T, 256), BF16),
        jax.ShapeDtypeStruct((B, IDX_HEADS, T), F32),
        jax.ShapeDtypeStruct((B, T, 256), F32),
        jax.ShapeDtypeStruct((B, T, 256), F32),
        jax.ShapeDtypeStruct((B, T, 512), BF16),
        jax.ShapeDtypeStruct((B, T, 256), F32),
        jax.ShapeDtypeStruct((B, T, 512), F32),
        jax.ShapeDtypeStruct((B, T, 512), BF16),
        jax.ShapeDtypeStruct((B, nT, 8, LANES), F32),
    ]
    out_specs = [tok(1024), tok(1024), tok(256), tok(256), tok(256), tok(IDX_DIM), tok(256),
                 pl.BlockSpec((None, IDX_HEADS, tt), lambda b, i: (b, 0, i)),
                 tok(256), tok(256), tok(512), tok(256), tok(512), tok(512),
                 pl.BlockSpec((None, None, 8, LANES), lambda b, i: (b, i, 0, 0))]
    return pl.pallas_call(
        functools.partial(_post_kernel, tt=tt, pos0=pos0),
        out_shape=out_shape,
        grid=(B, nT),
        in_specs=[tok(W), tab, tab, tab, tab, tab,
                  full((1, LANES)), full((1, LANES)), full((1, LANES)),
                  full((LANES, 256)), full((1, 256)),
                  full((len(C_WINDOWS), C_GROUP, C_GROUP)), full((1, C_WIDTH)),
                  pl.BlockSpec((None, HIST_ROWS, C_WIDTH), lambda b, i: (b, 0, 0))],
        out_specs=out_specs,
        scratch_shapes=[pltpu.VMEM((tt + HIST_ROWS, C_WIDTH), F32), pltpu.VMEM((HIST_ROWS, C_WIDTH), F32)],
        compiler_params=_cparams(("arbitrary", "arbitrary")),
        name="post",
    )(P, *tabs, gq, gk, gki, w_alpha, b_alpha, pool_w, pool_scale, hist0)


def _attn_kernel(*refs, tq, kb_size, causal, n_keys, topk, has_bound):
    q_ref, qi_ref, wt_ref, k_ref, v_ref, ki_ref = refs[:6]
    ksq_ref = refs[6] if has_bound else None
    o_ref, key_ref, plane_ref, acc_ref, m_ref, l_ref = refs[-6:]
    i = pl.program_id(1)

    @pl.when(i == 0)
    def _():
        plane_ref[...] = jnp.zeros(plane_ref.shape, jnp.int32)

    n_slabs = IDX_HEADS // 2
    if causal:
        nkb = ((i + 1) * tq + kb_size - 1) // kb_size
        t_abs = i * tq + lax.broadcasted_iota(jnp.int32, (1, tq), 1)
        limit = (t_abs // CHUNK + 1) * CHUNK
    else:
        nkb = -(-n_keys // kb_size)
        limit = jnp.full((1, tq), n_keys, jnp.int32)

    q2 = jnp.concatenate([qi_ref[:, g * LANES:(g + 1) * LANES] for g in range(n_slabs)], axis=0)

    def idx_body(kb, carry):
        for sb in range(kb_size // IDX_SUB):
            off = pl.multiple_of(kb * kb_size + sb * IDX_SUB, IDX_SUB)
            ye = _dot_nt(ki_ref[pl.ds(off, IDX_SUB), 0:LANES], q2)
            yo = _dot_nt(ki_ref[pl.ds(off, IDX_SUB), LANES:2 * LANES], q2)
            score = jnp.zeros((IDX_SUB, tq), F32)
            for g in range(n_slabs):
                score = score + jnp.maximum(ye[:, g * tq:(g + 1) * tq], 0.0) * wt_ref[2 * g:2 * g + 1, :]
                score = score + jnp.maximum(yo[:, g * tq:(g + 1) * tq], 0.0) * wt_ref[2 * g + 1:2 * g + 2, :]
            s_idx = off + lax.broadcasted_iota(jnp.int32, (IDX_SUB, 1), 0)
            bits = pltpu.bitcast(score, jnp.int32)
            key = bits ^ ((bits >> 31) & jnp.int32(0x7FFFFFFF))
            key = jnp.where(s_idx < limit, key, jnp.int32(INT_MIN))
            key_ref[pl.ds(off, IDX_SUB), :] = key
            ukey = key ^ jnp.int32(INT_MIN)
            planes = _bit_transpose32([ukey[8 * v:8 * v + 8, :] for v in range(32)])
            grp = pl.multiple_of((kb * kb_size + sb * IDX_SUB) // 32, 8)
            for p in range(32):
                plane_ref[p, pl.ds(grp, 8), :] = planes[p]
        return carry

    lax.fori_loop(0, nkb, idx_body, 0)

    n_words = key_ref.shape[0] // 32
    word_row = lax.broadcasted_iota(jnp.int32, (n_words, tq), 0)
    in_play = jnp.where(word_row < nkb * (kb_size // 32), jnp.int32(-1), jnp.int32(0))

    def bit_body(p, carry):
        cand, above, ukth = carry
        plane = plane_ref[p]
        ones = cand & plane
        c1 = jnp.sum(lax.population_count(ones).reshape(n_words // 8, 8, tq), axis=0)
        c1 = jnp.sum(c1, axis=0, keepdims=True)
        take = above + c1 >= topk
        cand = jnp.where(take, ones, cand & ~plane)
        above = jnp.where(take, above, above + c1)
        ukth = jnp.where(take, ukth | lax.shift_left(jnp.int32(1), 31 - p), ukth)
        return cand, above, ukth

    zero_row = jnp.zeros((1, tq), jnp.int32)
    _, _, ukth = lax.fori_loop(0, 32, bit_body, (in_play, zero_row, zero_row))
    tau = jnp.maximum(ukth ^ jnp.int32(INT_MIN), jnp.int32(INT_MIN + 1))

    def count32(pred):
        def body(kb, acc):
            off = pl.multiple_of(kb * kb_size, kb_size)
            s_idx = off + lax.broadcasted_iota(jnp.int32, (kb_size, 1), 0)
            hit = jnp.where(pred(key_ref[pl.ds(off, kb_size), :], s_idx), 1, 0)
            return acc + jnp.sum(hit.reshape(kb_size // 8, 8, tq), axis=0)

        return jnp.sum(lax.fori_loop(0, nkb, body, jnp.zeros((8, tq), jnp.int32)), axis=0, keepdims=True)

    n_ge = count32(lambda k, s: k >= tau)

    @pl.when(jnp.max(n_ge) > topk)
    def _():
        quota = topk - count32(lambda k, s: k > tau)
        idx_bits = int(key_ref.shape[0]).bit_length()

        def bit_body(bi, cut):
            cand = cut | lax.shift_left(jnp.int32(1), idx_bits - 1 - bi)
            kept = count32(lambda k, s: jnp.where(k == tau, s, cand) < cand)
            return jnp.where(kept <= quota, cand, cut)

        cut = lax.fori_loop(0, idx_bits, bit_body, jnp.zeros((1, tq), jnp.int32))

        def demote(kb, carry):
            off = pl.multiple_of(kb * kb_size, kb_size)
            s_idx = off + lax.broadcasted_iota(jnp.int32, (kb_size, 1), 0)
            blk = key_ref[pl.ds(off, kb_size), :]
            key_ref[pl.ds(off, kb_size), :] = jnp.where(jnp.where(blk == tau, s_idx, -1) >= cut, tau - 1, blk)
            return carry

        lax.fori_loop(0, nkb, demote, 0)

    hpg = A_HEADS // A_KV_HEADS
    qgs = [jnp.concatenate([q_ref[:, (g * hpg + h) * LANES:(g * hpg + h + 1) * LANES] for h in range(hpg)], axis=0)
           for g in range(A_KV_HEADS)]
    l_ref[...] = jnp.zeros(l_ref.shape, F32)
    acc_ref[...] = jnp.zeros(acc_ref.shape, F32)

    def masked_scores(kb, g, sel, fill, shift):
        off = pl.multiple_of(kb * kb_size, kb_size)
        s = _dot_nt(k_ref[pl.ds(off, kb_size), g * LANES:(g + 1) * LANES], qgs[g])
        return jnp.concatenate([jnp.where(sel, shift(s[:, h * tq:(h + 1) * tq], h), fill) for h in range(hpg)], axis=1)

    def add_pv(kb, g, p):
        off = pl.multiple_of(kb * kb_size, kb_size)
        return _dot_tn(v_ref[pl.ds(off, kb_size), g * LANES:(g + 1) * LANES], p.astype(BF16))

    def selected(kb):
        off = pl.multiple_of(kb * kb_size, kb_size)
        return key_ref[pl.ds(off, kb_size), :] >= tau

    def online():
        m_ref[...] = jnp.full(m_ref.shape, -1e30, F32)

        def body(kb, carry):
            sel = selected(kb)
            for g in range(A_KV_HEADS):
                s = masked_scores(kb, g, sel, -jnp.inf, lambda x, h: x)
                m_old = m_ref[g]
                m_new = jnp.maximum(m_old, jnp.max(s, axis=0, keepdims=True))
                alpha = jnp.exp2(m_old - m_new)
                p = jnp.exp2(s - m_new)
                l_ref[g] = alpha * l_ref[g] + jnp.sum(p, axis=0, keepdims=True)
                acc_ref[g] = acc_ref[g] * alpha + add_pv(kb, g, p)
                m_ref[g] = m_new
            return carry

        lax.fori_loop(0, nkb, body, 0)

    def bounded(bounds):
        def body(kb, carry):
            sel = selected(kb)
            for g in range(A_KV_HEADS):
                b = bounds[g]
                p = masked_scores(kb, g, sel, 0.0, lambda x, h: jnp.exp2(x - b[:, h * tq:(h + 1) * tq]))
                l_ref[g] = l_ref[g] + jnp.sum(p, axis=0, keepdims=True)
                acc_ref[g] = acc_ref[g] + add_pv(kb, g, p)
            return carry

        lax.fori_loop(0, nkb, body, 0)

    if ksq_ref is None:
        online()
    else:
        kmax = jnp.sqrt(jnp.max(ksq_ref[...]))
        ones = jnp.ones((8, LANES), BF16)
        bounds = [jnp.sqrt(_dot_nt(ones, qg * qg)[0:1, :]) * (kmax * BOUND_MARGIN) for qg in qgs]
        safe = jnp.max(jnp.maximum(bounds[0], bounds[1])) <= BOUND_SAFE

        @pl.when(safe)
        def _():
            bounded(bounds)

        @pl.when(jnp.logical_not(safe))
        def _():
            online()

    for g in range(A_KV_HEADS):
        o = acc_ref[g] / l_ref[g]
        for h in range(hpg):
            hh = g * hpg + h
            o_ref[:, hh * LANES:(hh + 1) * LANES] = o[:, h * tq:(h + 1) * tq].T.astype(o_ref.dtype)


def _attn(q, qi, wt, kb, vb, kieo, ksq, tq, kb_size, causal, n_keys, topk):
    B, T, _ = q.shape
    S = kb.shape[1]
    hpg = A_HEADS // A_KV_HEADS
    kern = functools.partial(_attn_kernel, tq=tq, kb_size=kb_size, causal=causal, n_keys=n_keys, topk=topk,
                             has_bound=ksq is not None)
    in_specs = [pl.BlockSpec((None, tq, 1024), lambda b, i: (b, i, 0)),
                pl.BlockSpec((None, tq, 1024), lambda b, i: (b, i, 0)),
                pl.BlockSpec((None, IDX_HEADS, tq), lambda b, i: (b, 0, i)),
                pl.BlockSpec((None, S, 256), lambda b, i: (b, 0, 0)),
                pl.BlockSpec((None, S, 256), lambda b, i: (b, 0, 0)),
                pl.BlockSpec((None, S, 256), lambda b, i: (b, 0, 0))]
    args = [q, qi, wt, kb, vb, kieo]
    if ksq is not None:
        in_specs.append(pl.BlockSpec((None,) + ksq.shape[1:], lambda b, i: (b, 0, 0, 0)))
        args.append(ksq)
    return pl.pallas_call(
        kern,
        out_shape=jax.ShapeDtypeStruct((B, T, 1024), BF16),
        grid=(B, T // tq),
        in_specs=in_specs,
        out_specs=pl.BlockSpec((None, tq, 1024), lambda b, i: (b, i, 0)),
        scratch_shapes=[pltpu.VMEM((S, tq), jnp.int32),
                        pltpu.VMEM((32, S // 32, tq), jnp.int32),
                        pltpu.VMEM((A_KV_HEADS, A_HEAD_DIM, hpg * tq), F32),
                        pltpu.VMEM((A_KV_HEADS, 1, hpg * tq), F32),
                        pltpu.VMEM((A_KV_HEADS, 1, hpg * tq), F32)],
        compiler_params=_cparams(("arbitrary", "arbitrary")),
        name="attn",
    )(*args)


def _gla_consts(C):
    nl = int(np.log2(C))
    M = np.zeros((nl + 2, C, C), np.float32)
    masks = np.zeros((nl + 1, C, C), np.float32)
    for j in range(nl):
        m = 1 << j
        for t in range(C):
            blk = (t // (2 * m)) * 2 * m
            mid = blk + m
            if t >= mid:
                M[j, t, mid:t + 1] = 1
                masks[j, t, blk:mid] = 1
            else:
                M[j, t, t + 1:mid] = 1
    for t in range(C):
        M[nl, t, :t + 1] = 1
        M[nl + 1, t, t + 1:] = 1
    masks[nl] = np.eye(C)
    return M.reshape((nl + 2) * C, C), np.concatenate([masks, masks], axis=1)


def _gla_kernel(q_ref, k_ref, v_ref, lg_ref, gate_ref, mcat_ref, masks_ref, gn_ref, s0_ref,
                o_ref, sout_ref, s_ref, *, tt, C):
    i = pl.program_id(1)
    nl = int(np.log2(C))

    @pl.when(i == 0)
    def _():
        s_ref[...] = s0_ref[...]

    lane = lax.broadcasted_iota(jnp.int32, (1, LANES), 1)
    low = lane < B_KEY_DIM
    row = lax.broadcasted_iota(jnp.int32, (C, 1), 0)
    rr = lax.broadcasted_iota(jnp.int32, (LANES, 2 * B_VAL_DIM), 0)
    cc = lax.broadcasted_iota(jnp.int32, (LANES, 2 * B_VAL_DIM), 1)
    blockdiag = (rr < B_KEY_DIM) == (cc < B_VAL_DIM)
    gn = gn_ref[...]

    def stack2(x):
        return jnp.concatenate([jnp.where(low, x, 0.0), jnp.where(low, 0.0, x)], axis=0).astype(BF16)

    def chunk_body(c, carry):
        r0 = pl.multiple_of(c * C, C)
        qc = q_ref[pl.ds(r0, C), :]
        kc = k_ref[pl.ds(r0, C), :]
        vc = v_ref[pl.ds(r0, C), :]
        lgc = lg_ref[pl.ds(r0, C), :]
        hi = lgc.astype(BF16)
        r1 = lgc - hi.astype(F32)
        mid = r1.astype(BF16)
        lo = (r1 - mid.astype(F32)).astype(BF16)
        mcat = mcat_ref[...]
        E = jnp.exp(_dot(mcat, hi) + _dot(mcat, mid) + _dot(mcat, lo))
        e_b = E[nl * C:(nl + 1) * C, :]
        e_k = E[(nl + 1) * C:(nl + 2) * C, :]
        qt = qc * e_b
        kt = kc * e_k
        e_last = e_b[C - 1:C, :]
        for g in range(B_HEADS // 2):
            sl = slice(g * LANES, (g + 1) * LANES)
            att = _dot_nt(stack2(qc[:, sl]), kc[:, sl].astype(BF16)) * masks_ref[nl]
            for j in range(nl):
                right = ((row >> j) & 1) == 1
                x = jnp.where(right, qc[:, sl], kc[:, sl]) * E[j * C:(j + 1) * C, sl]
                att = att + _dot_nt(stack2(x), x.astype(BF16)) * masks_ref[j]
            att = att.astype(BF16)
            vg = vc[:, g * 2 * B_VAL_DIM:(g + 1) * 2 * B_VAL_DIM]
            sg = s_ref[g]
            o_intra = jnp.concatenate([_dot(att[0:C], vg[:, 0:B_VAL_DIM]),
                                       _dot(att[C:2 * C], vg[:, B_VAL_DIM:2 * B_VAL_DIM])], axis=1)
            o = o_intra + _dot(qt[:, sl].astype(BF16), sg.astype(BF16))
            upd = _dot_tn(kt[:, sl].astype(BF16), vg)
            dec = jnp.broadcast_to(e_last[:, sl], (LANES, LANES)).T
            dec = jnp.concatenate([dec, dec], axis=1)
            s_ref[g] = dec * sg + jnp.where(blockdiag, upd, 0.0)
            for hh in range(2):
                h = 2 * g + hh
                oh = o[:, hh * B_VAL_DIM:(hh + 1) * B_VAL_DIM]
                r = lax.rsqrt(jnp.mean(oh * oh, axis=-1, keepdims=True) + EPS)
                gsl = slice(h * B_VAL_DIM, (h + 1) * B_VAL_DIM)
                o_ref[pl.ds(r0, C), gsl] = (oh * r * gn * gate_ref[pl.ds(r0, C), gsl]).astype(o_ref.dtype)
        return carry

    lax.fori_loop(0, tt // C, chunk_body, 0)

    @pl.when(i == pl.num_programs(1) - 1)
    def _():
        sout_ref[...] = s_ref[...]


def _gla(bq, bk, bv, lg, gate, gn, s0, tt, C):
    B, T, _ = bq.shape
    mcat, masks = _gla_consts(C)
    nl = int(np.log2(C))
    tok = lambda c: pl.BlockSpec((None, tt, c), lambda b, i: (b, i, 0))
    full = lambda shape: pl.BlockSpec(shape, lambda b, i: (0,) * len(shape))
    st = pl.BlockSpec((None, 2, LANES, 2 * B_VAL_DIM), lambda b, i: (b, 0, 0, 0))
    return pl.pallas_call(
        functools.partial(_gla_kernel, tt=tt, C=C),
        out_shape=[jax.ShapeDtypeStruct((B, T, 512), BF16),
                   jax.ShapeDtypeStruct((B, 2, LANES, 2 * B_VAL_DIM), F32)],
        grid=(B, T // tt),
        in_specs=[tok(256), tok(256), tok(512), tok(256), tok(512),
                  full(((nl + 2) * C, C)), full((nl + 1, 2 * C, C)), full((1, B_VAL_DIM)), st],
        out_specs=[tok(512), st],
        scratch_shapes=[pltpu.VMEM((2, LANES, 2 * B_VAL_DIM), F32)],
        compiler_params=_cparams(("arbitrary", "arbitrary")),
        name="gla",
    )(bq, bk, bv, lg, gate, jnp.asarray(mcat, BF16), jnp.asarray(masks, F32), gn, s0)


def _out_kernel(x_ref, oa_ref, ob_ref, oc_ref, wo_ref, g2_ref, h_ref, m_ref):
    acc = _dot(oa_ref[...], wo_ref[0:1024, :])
    acc = acc + _dot(ob_ref[...], wo_ref[1024:1536, :])
    acc = acc + _dot(oc_ref[...], wo_ref[1536:2048, :])
    h = x_ref[...] + acc
    h_ref[...] = h
    ms = jnp.mean(h * h, axis=-1, keepdims=True)
    m_ref[...] = (h * lax.rsqrt(ms + EPS) * g2_ref[...]).astype(BF16)


def _out_proj(x, oa, ob, oc, wo, g2, bm):
    M, D = x.shape
    row = lambda c: pl.BlockSpec((bm, c), lambda i: (i, 0))
    return pl.pallas_call(
        _out_kernel,
        out_shape=[jax.ShapeDtypeStruct((M, D), F32), jax.ShapeDtypeStruct((M, D), BF16)],
        grid=(M // bm,),
        in_specs=[row(D), row(1024), row(512), row(512),
                  pl.BlockSpec((D, D), lambda i: (0, 0)), pl.BlockSpec((1, D), lambda i: (0, 0))],
        out_specs=[row(D), row(D)],
        compiler_params=_cparams(("arbitrary",)),
        name="out_proj",
    )(x, oa, ob, oc, wo, g2.reshape(1, D))


def _ffn_kernel(h_ref, m_ref, wg_ref, wu_ref, wd_ref, y_ref):
    @pl.when(pl.program_id(1) == 0)
    def _():
        y_ref[...] = h_ref[...]

    m = m_ref[...]
    gate = _dot(m, wg_ref[...])
    up = _dot(m, wu_ref[...])
    a = (gate * (1.0 / (1.0 + jnp.exp(-gate))) * up).astype(BF16)
    y_ref[...] += _dot(a, wd_ref[...])


def _ffn(h, m, wg, wu, wd, bm, bf):
    M, D = h.shape
    Fh = wg.shape[1]
    return pl.pallas_call(
        _ffn_kernel,
        out_shape=jax.ShapeDtypeStruct((M, D), F32),
        grid=(M // bm, Fh // bf),
        in_specs=[pl.BlockSpec((bm, D), lambda i, j: (i, 0)),
                  pl.BlockSpec((bm, D), lambda i, j: (i, 0)),
                  pl.BlockSpec((D, bf), lambda i, j: (0, j)),
                  pl.BlockSpec((D, bf), lambda i, j: (0, j)),
                  pl.BlockSpec((bf, D), lambda i, j: (j, 0))],
        out_specs=pl.BlockSpec((bm, D), lambda i, j: (i, 0)),
        compiler_params=_cparams(("arbitrary", "arbitrary")),
        name="ffn",
    )(h, m, wg, wu, wd)


def _rope_tables(pos0, T):
    pos = (pos0 + jnp.arange(T, dtype=jnp.int32)).astype(F32)

    def cs(half):
        inv = ROPE_THETA ** (-jnp.arange(half, dtype=F32) / half)
        ang = pos[:, None] * inv[None, :]
        return jnp.cos(ang), jnp.sin(ang)

    c64, s64 = cs(A_HEAD_DIM // 2)
    c32, s32 = cs(IDX_DIM // 2)
    z32 = jnp.zeros_like(s32)
    return (jnp.concatenate([c64, c64], axis=1), jnp.concatenate([-s64, s64], axis=1),
            jnp.concatenate([c32, c32, c32, c32], axis=1),
            jnp.concatenate([-s32, z32, -s32, z32], axis=1),
            jnp.concatenate([z32, s32, z32, s32], axis=1))


def _pack_w_in(w_in):
    cols = []
    for name in _PACK_ORDER:
        so, w, _, pw = _LAYOUT[name]
        piece = w_in[:, :, so:so + w]
        if pw > w:
            piece = jnp.pad(piece, ((0, 0), (0, 0), (0, pw - w)))
        cols.append(piece)
    return jnp.concatenate(cols, axis=-1).astype(BF16)


def _state_to_groups(s):
    B = s.shape[0]
    s = s.reshape(B, 2, 2, B_KEY_DIM, B_VAL_DIM)
    z = jnp.zeros_like(s[:, :, 0])
    top = jnp.concatenate([s[:, :, 0], z], axis=-1)
    bot = jnp.concatenate([z, s[:, :, 1]], axis=-1)
    return jnp.concatenate([top, bot], axis=-2)


def _groups_to_state(sg):
    B = sg.shape[0]
    h0 = sg[:, :, :B_KEY_DIM, :B_VAL_DIM]
    h1 = sg[:, :, B_KEY_DIM:, B_VAL_DIM:]
    return jnp.stack([h0, h1], axis=2).reshape(B, B_HEADS, B_KEY_DIM, B_VAL_DIM)


def _layer(x, pos0, cache, prm):
    (g1, w_in_p, gq, gk, gki, w_alpha, b_alpha, g_gla, pool_w, pool_scale, w_o, g2, w_gate, w_up, w_down) = prm
    B, T, D = x.shape
    M = B * T
    x2 = x.reshape(M, D)
    bm = min(512, M)
    P = _norm_matmul(x2, g1, w_in_p, bm, PACKED_WIDTH // 3).reshape(B, T, PACKED_WIDTH)

    tt = min(512, T)
    tabs = _rope_tables(pos0, T)
    gki_p = jnp.pad(gki, (0, LANES - IDX_DIM)).reshape(1, LANES)
    w_alpha_p = jnp.pad(w_alpha, ((0, LANES - B_GATE_RANK), (0, 0))).astype(BF16)
    if cache is None:
        hist0 = jnp.zeros((B, HIST_ROWS, C_WIDTH), F32)
    else:
        hist0 = jnp.pad(cache[4], ((0, 0), (HIST_ROWS - POOL_HIST, 0), (0, 0)))
    (q, qi, k32, kb, vb, ki32, kieo, wt, bq, bk, bv, lg, gate, oc, ksq) = _post(
        P, tabs, gq.reshape(1, LANES), gk.reshape(1, LANES), gki_p, w_alpha_p, b_alpha.reshape(1, -1),
        pool_w.astype(BF16), pool_scale.reshape(1, -1), hist0, tt, pos0)

    v_off, _ = _seg("a_v")
    v32 = P[:, :, v_off:v_off + 256]
    u_off, _ = _seg("c_u")
    pool_state = P[:, T - POOL_HIST:, u_off:u_off + C_WIDTH]

    if cache is None:
        tq = LANES
        oa = _attn(q, qi, wt, kb, vb, kieo, ksq, tq, 512, True, T, min(TOPK_MAX, T // 4))
        s0 = jnp.zeros((B, 2, LANES, 2 * B_VAL_DIM), F32)
        ob, sg = _gla(bq, bk, bv, lg, gate, g_gla.reshape(1, -1), s0, tt, CHUNK)
    else:
        ck, cv, cki, c_gla, _ = cache
        past = ck.shape[1]
        L = past + T
        kb_size = IDX_SUB
        S = -(-L // kb_size) * kb_size
        padk = lambda a: jnp.pad(a, ((0, 0), (0, S - L), (0, 0)))
        k_all = padk(jnp.concatenate([ck.reshape(B, past, 256).astype(BF16), kb], axis=1))
        v_all = padk(jnp.concatenate([cv.reshape(B, past, 256).astype(BF16), vb], axis=1))
        zc = jnp.zeros((B, past, IDX_DIM), BF16)
        cki_b = cki.astype(BF16)
        ki_all = padk(jnp.concatenate([jnp.concatenate([cki_b, zc, zc, cki_b], axis=-1), kieo], axis=1))
        tq = LANES
        padq = lambda a: jnp.pad(a, ((0, 0), (0, tq - T), (0, 0)))
        wt_p = jnp.pad(wt, ((0, 0), (0, 0), (0, tq - T)))
        oa = _attn(padq(q), padq(qi), wt_p, k_all, v_all, ki_all, None, tq, kb_size, False, L,
                   min(TOPK_MAX, L // 4))[:, :T]
        ob, sg = _gla(bq, bk, bv, lg, gate, g_gla.reshape(1, -1), _state_to_groups(c_gla), tt, T)

    h, m = _out_proj(x2, oa.reshape(M, -1), ob.reshape(M, -1), oc.reshape(M, -1), w_o, g2, bm)
    y = _ffn(h, m, w_gate, w_up, w_down, bm, 512)
    state = (k32.reshape(B, T, A_KV_HEADS, A_HEAD_DIM), v32.reshape(B, T, A_KV_HEADS, A_HEAD_DIM),
             ki32, _groups_to_state(sg), pool_state)
    return y.reshape(B, T, D), state


def kernel(x_prompt, x_sample, cache_k, cache_v, cache_kidx, state_gla, state_pool,
           norm1, w_in, q_norm, k_norm, kidx_norm, w_alpha, b_alpha, gla_norm, pool_w, pool_scale,
           w_o, norm2, w_gate, w_up, w_down):
    depth = w_in.shape[0]
    past = cache_k.shape[2]
    w_in_p = _pack_w_in(w_in)
    w_o_b, w_gate_b, w_up_b, w_down_b = (w.astype(BF16) for w in (w_o, w_gate, w_up, w_down))
    yp, ys = x_prompt, x_sample
    st_p, st_s = [], []
    for l in range(depth):
        prm = (norm1[l], w_in_p[l], q_norm[l], k_norm[l], kidx_norm[l], w_alpha[l], b_alpha[l], gla_norm[l],
               pool_w[l], pool_scale[l], w_o_b[l], norm2[l], w_gate_b[l], w_up_b[l], w_down_b[l])
        yp, sp = _layer(yp, 0, None, prm)
        ys, ss = _layer(ys, past, (cache_k[l], cache_v[l], cache_kidx[l], state_gla[l], state_pool[l]), prm)
        st_p.append(sp)
        st_s.append(ss)
    stk = lambda sts, i: jnp.stack([s[i] for s in sts], axis=0)
    return (yp, ys,
            stk(st_p, 0), stk(st_p, 1), stk(st_p, 2), stk(st_p, 3), stk(st_p, 4),
            stk(st_s, 0), stk(st_s, 1), stk(st_s, 2), stk(st_s, 3), stk(st_s, 4))
```

```python
import functools

import numpy as np
import jax
import jax.numpy as jnp
from jax import lax
from jax.experimental import pallas as pl
from jax.experimental.pallas import tpu as pltpu

F32 = jnp.float32
BF16 = jnp.bfloat16

EPS = 1e-6
CHUNK = 64
TOPK_MAX = 256
ROPE_THETA = 10000.0
A_HEAD_DIM = 128
A_HEADS = 8
A_KV_HEADS = 2
IDX_HEADS = 16
IDX_DIM = 64
IDX_W_SCALE = (IDX_HEADS ** -0.5) * (IDX_DIM ** -0.5)
B_HEADS = 4
B_KEY_DIM = 64
B_VAL_DIM = 128
B_GATE_RANK = 16
B_GATE_TEMP = 16.0
C_WINDOWS = (2, 4, 8, 16)
C_WIDTH = 512
C_GROUP = 128
POOL_HIST = 15
HIST_ROWS = 16

LANES = 128
VMEM_LIMIT = 60 * 1024 * 1024

INT_MIN = -(2 ** 31)
IDX_SUB = 256
ATT_KB = (2048, 512)
Q_SCALE = (A_HEAD_DIM ** -0.5) * float(np.log2(np.e))
BOUND_MARGIN = 1.02
BOUND_SAFE = 60.0

_SRC_SIZES = (("a_q", 1024), ("a_k", 256), ("a_v", 256), ("a_qi", 1024), ("a_ki", 64), ("a_wi", 16),
              ("b_q", 256), ("b_k", 256), ("b_v", 512), ("b_a", 16), ("b_g", 512), ("c_u", 512))
_PACK_ORDER = ("a_q", "a_qi", "b_v", "b_g", "c_u", "a_k", "a_v", "b_q", "b_k", "a_ki", "a_wi", "b_a")


def _build_layout():
    src, off = {}, 0
    for name, w in _SRC_SIZES:
        src[name] = (off, w)
        off += w
    lay, poff = {}, 0
    for name in _PACK_ORDER:
        so, w = src[name]
        pw = -(-w // LANES) * LANES
        lay[name] = (so, w, poff, pw)
        poff += pw
    return lay, poff


_LAYOUT, PACKED_WIDTH = _build_layout()


def _seg(name):
    _, _, poff, pw = _LAYOUT[name]
    return poff, pw


def _dot(a, b):
    return jnp.dot(a, b, preferred_element_type=F32)


def _dot_nt(a, b):
    return lax.dot_general(a, b, (((1,), (1,)), ((), ())), preferred_element_type=F32)


def _dot_tn(a, b):
    return lax.dot_general(a, b, (((0,), (0,)), ((), ())), preferred_element_type=F32)


def _bit_transpose32(rows):
    a = list(rows)
    j, m = 16, 0x0000FFFF
    while j:
        k = 0
        while k < 32:
            t = (a[k] ^ lax.shift_right_logical(a[k + j], jnp.int32(j))) & jnp.int32(m)
            a[k] = a[k] ^ t
            a[k + j] = a[k + j] ^ lax.shift_left(t, jnp.int32(j))
            k = (k + j + 1) & ~j
        j >>= 1
        m = (m ^ (m << j)) & 0xFFFFFFFF
    return a


def _cparams(sem):
    return pltpu.CompilerParams(dimension_semantics=sem, vmem_limit_bytes=VMEM_LIMIT)


def _norm_matmul_kernel(x_ref, g_ref, w_ref, o_ref, xn_ref):
    @pl.when(pl.program_id(1) == 0)
    def _():
        x = x_ref[...]
        ms = jnp.mean(x * x, axis=-1, keepdims=True)
        xn_ref[...] = (x * lax.rsqrt(ms + EPS) * g_ref[...]).astype(BF16)

    o_ref[...] = _dot(xn_ref[...], w_ref[...])


def _norm_matmul(x, g, w, bm, bn):
    M, D = x.shape
    N = w.shape[1]
    return pl.pallas_call(
        _norm_matmul_kernel,
        out_shape=jax.ShapeDtypeStruct((M, N), F32),
        grid=(M // bm, N // bn),
        in_specs=[pl.BlockSpec((bm, D), lambda i, j: (i, 0)),
                  pl.BlockSpec((1, D), lambda i, j: (0, 0)),
                  pl.BlockSpec((D, bn), lambda i, j: (0, j))],
        out_specs=pl.BlockSpec((bm, bn), lambda i, j: (i, j)),
        scratch_shapes=[pltpu.VMEM((bm, D), BF16)],
        compiler_params=_cparams(("arbitrary", "arbitrary")),
        name="norm_matmul",
    )(x, g.reshape(1, D), w)


def _post_kernel(p_ref, cosa_ref, sina_ref, cosb_ref, sinlo_ref, sinhi_ref, gq_ref, gk_ref, gki_ref,
                 walpha_ref, balpha_ref, poolw_ref, pscale_ref, hist0_ref,
                 q_ref, qi_ref, k32_ref, kb_ref, vb_ref, ki32_ref, kieo_ref, wt_ref,
                 bq_ref, bk_ref, bv_ref, lg_ref, gate_ref, oc_ref, ksq_ref,
                 uext_ref, hist_ref, *, tt, pos0):
    i = pl.program_id(1)

    def seg(name):
        off, w = _seg(name)
        return p_ref[:, off:off + w]

    cosa, sina = cosa_ref[...], sina_ref[...]

    def head_norm_rope(a, g):
        r = lax.rsqrt(jnp.mean(a * a, axis=-1, keepdims=True) + EPS)
        y = a * r * g
        return y * cosa + pltpu.roll(y, 64, 1) * sina

    aq = seg("a_q")
    gq = gq_ref[...]
    for h in range(A_HEADS):
        sl = slice(h * LANES, (h + 1) * LANES)
        q_ref[:, sl] = (head_norm_rope(aq[:, sl], gq) * Q_SCALE).astype(BF16)
    ak = seg("a_k")
    gk = gk_ref[...]
    ksq = jnp.zeros((tt, 1), F32)
    for h in range(A_KV_HEADS):
        sl = slice(h * LANES, (h + 1) * LANES)
        kk = head_norm_rope(ak[:, sl], gk)
        k32_ref[:, sl] = kk
        kb_ref[:, sl] = kk.astype(BF16)
        ksq = jnp.maximum(ksq, jnp.sum(kk * kk, axis=-1, keepdims=True))
    ksq_ref[...] = jnp.broadcast_to(jnp.max(ksq, axis=0, keepdims=True), ksq_ref.shape)
    vb_ref[...] = seg("a_v").astype(BF16)

    cosb, sinlo, sinhi = cosb_ref[...], sinlo_ref[...], sinhi_ref[...]

    def rope64(y):
        return y * cosb + pltpu.roll(y, 96, 1) * sinlo + pltpu.roll(y, 32, 1) * sinhi

    aqi = seg("a_qi")
    for gidx in range(IDX_HEADS // 2):
        sl = slice(gidx * LANES, (gidx + 1) * LANES)
        qi_ref[:, sl] = rope64(aqi[:, sl]).astype(BF16)
    aki = seg("a_ki")
    rki = lax.rsqrt(jnp.sum(aki * aki, axis=-1, keepdims=True) * (1.0 / IDX_DIM) + EPS)
    ki = rope64(aki * rki * gki_ref[...])
    ki32_ref[...] = ki[:, :IDX_DIM]
    kieo_ref[:, 0:LANES] = ki.astype(BF16)
    kieo_ref[:, LANES:2 * LANES] = pltpu.roll(ki, 64, 1).astype(BF16)

    awi = seg("a_wi") * IDX_W_SCALE
    if tt % LANES == 0:
        for c in range(tt // LANES):
            wt_ref[:, c * LANES:(c + 1) * LANES] = awi[c * LANES:(c + 1) * LANES, :].T[:IDX_HEADS, :]
    else:
        padded = jnp.concatenate([awi, jnp.zeros((LANES - tt, LANES), F32)], axis=0)
        wt_ref[...] = padded.T[:IDX_HEADS, :tt]

    bq_ref[...] = seg("b_q") * (B_KEY_DIM ** -0.5)
    bk_ref[...] = seg("b_k")
    bv_ref[...] = seg("b_v").astype(BF16)
    z = _dot(seg("b_a").astype(BF16), walpha_ref[...]) + balpha_ref[...]
    lg_ref[...] = (jnp.minimum(z, 0.0) - jnp.log1p(jnp.exp(-jnp.abs(z)))) * (1.0 / B_GATE_TEMP)
    bg = seg("b_g")
    gate_ref[...] = bg * (1.0 / (1.0 + jnp.exp(-bg)))

    @pl.when(i == 0)
    def _():
        hist_ref[...] = hist0_ref[...]

    u = seg("c_u")
    uext_ref[0:HIST_ROWS, :] = hist_ref[...]
    uext_ref[HIST_ROWS:HIST_ROWS + tt, :] = u
    hist_ref[...] = uext_ref[tt:tt + HIST_ROWS, :]
    pos = pos0 + i * tt + lax.broadcasted_iota(jnp.int32, (tt, 1), 0)
    for gi, w in enumerate(C_WINDOWS):
        sl = slice(gi * C_GROUP, (gi + 1) * C_GROUP)
        s = uext_ref[:, sl]
        shift = 1
        while shift < w:
            s = s + pltpu.roll(s, shift, 0)
            shift *= 2
        cnt = jnp.minimum(w, pos + 1).astype(F32)
        d = s[HIST_ROWS:, :] / cnt - u[:, sl]
        y = _dot(d.astype(BF16), poolw_ref[gi]) * pscale_ref[:, sl]
        oc_ref[:, sl] = y.astype(BF16)


def _post(P, tabs, gq, gk, gki, w_alpha, b_alpha, pool_w, pool_scale, hist0, tt, pos0):
    B, T, W = P.shape
    nT = T // tt
    tok = lambda c: pl.BlockSpec((None, tt, c), lambda b, i: (b, i, 0))
    tab = pl.BlockSpec((tt, LANES), lambda b, i: (i, 0))
    full = lambda shape: pl.BlockSpec(shape, lambda b, i: (0,) * len(shape))
    out_shape = [
        jax.ShapeDtypeStruct((B, T, 1024), BF16),
        jax.ShapeDtypeStruct((B, T, 1024), BF16),
        jax.ShapeDtypeStruct((B, T, 256), F32),
        jax.ShapeDtypeStruct((B, T, 256), BF16),
        jax.ShapeDtypeStruct((B, T, 256), BF16),
        jax.ShapeDtypeStruct((B, T, IDX_DIM), F32),
        jax.ShapeDtypeStruct((B, T, 256), BF16),
        jax.ShapeDtypeStruct((B, IDX_HEADS, T), F32),
        jax.ShapeDtypeStruct((B, T, 256), F32),
        jax.ShapeDtypeStruct((B, T, 256), F32),
        jax.ShapeDtypeStruct((B, T, 512), BF16),
        jax.ShapeDtypeStruct((B, T, 256), F32),
        jax.ShapeDtypeStruct((B, T, 512), F32),
        jax.ShapeDtypeStruct((B, T, 512), BF16),
        jax.ShapeDtypeStruct((B, nT, 8, LANES), F32),
    ]
    out_specs = [tok(1024), tok(1024), tok(256), tok(256), tok(256), tok(IDX_DIM), tok(256),
                 pl.BlockSpec((None, IDX_HEADS, tt), lambda b, i: (b, 0, i)),
                 tok(256), tok(256), tok(512), tok(256), tok(512), tok(512),
                 pl.BlockSpec((None, None, 8, LANES), lambda b, i: (b, i, 0, 0))]
    return pl.pallas_call(
        functools.partial(_post_kernel, tt=tt, pos0=pos0),
        out_shape=out_shape,
        grid=(B, nT),
        in_specs=[tok(W), tab, tab, tab, tab, tab,
                  full((1, LANES)), full((1, LANES)), full((1, LANES)),
                  full((LANES, 256)), full((1, 256)),
                  full((len(C_WINDOWS), C_GROUP, C_GROUP)), full((1, C_WIDTH)),
                  pl.BlockSpec((None, HIST_ROWS, C_WIDTH), lambda b, i: (b, 0, 0))],
        out_specs=out_specs,
        scratch_shapes=[pltpu.VMEM((tt + HIST_ROWS, C_WIDTH), F32), pltpu.VMEM((HIST_ROWS, C_WIDTH), F32)],
        compiler_params=_cparams(("arbitrary", "arbitrary")),
        name="post",
    )(P, *tabs, gq, gk, gki, w_alpha, b_alpha, pool_w, pool_scale, hist0)


def _attn_kernel(*refs, tq, kb_sizes, causal, n_keys, topk, has_bound):
    q_ref, qi_ref, wt_ref, k_ref, v_ref, ki_ref = refs[:6]
    ksq_ref = refs[6] if has_bound else None
    o_ref, key_ref, plane_ref, acc_ref, m_ref, l_ref = refs[-6:]
    i = pl.program_id(1)

    @pl.when(i == 0)
    def _():
        plane_ref[...] = jnp.zeros(plane_ref.shape, jnp.int32)

    n_slabs = IDX_HEADS // 2
    kb_big, kb_small = kb_sizes
    if causal:
        n_vis = (i + 1) * tq
        t_abs = i * tq + lax.broadcasted_iota(jnp.int32, (1, tq), 1)
        limit = (t_abs // CHUNK + 1) * CHUNK
    else:
        n_vis = n_keys
        limit = jnp.full((1, tq), n_keys, jnp.int32)
    n_big = n_vis // kb_big
    n_small = (n_vis - n_big * kb_big + kb_small - 1) // kb_small
    n_rows = n_big * kb_big + n_small * kb_small

    def for_blocks(body, carry):
        def run(size, first, count, c):
            return lax.fori_loop(0, count, lambda j, cc: body(pl.multiple_of(first + j * size, size), size, cc), c)

        return run(kb_small, n_big * kb_big, n_small, run(kb_big, 0, n_big, carry))

    q2 = jnp.concatenate([qi_ref[:, g * LANES:(g + 1) * LANES] for g in range(n_slabs)], axis=0)

    def idx_body(off0, size, carry):
        for sb in range(size // IDX_SUB):
            off = pl.multiple_of(off0 + sb * IDX_SUB, IDX_SUB)
            ye = _dot_nt(ki_ref[pl.ds(off, IDX_SUB), 0:LANES], q2)
            yo = _dot_nt(ki_ref[pl.ds(off, IDX_SUB), LANES:2 * LANES], q2)
            score = jnp.zeros((IDX_SUB, tq), F32)
            for g in range(n_slabs):
                score = score + jnp.maximum(ye[:, g * tq:(g + 1) * tq], 0.0) * wt_ref[2 * g:2 * g + 1, :]
                score = score + jnp.maximum(yo[:, g * tq:(g + 1) * tq], 0.0) * wt_ref[2 * g + 1:2 * g + 2, :]
            s_idx = off + lax.broadcasted_iota(jnp.int32, (IDX_SUB, 1), 0)
            bits = pltpu.bitcast(score, jnp.int32)
            key = bits ^ ((bits >> 31) & jnp.int32(0x7FFFFFFF))
            key = jnp.where(s_idx < limit, key, jnp.int32(INT_MIN))
            key_ref[pl.ds(off, IDX_SUB), :] = key
            ukey = key ^ jnp.int32(INT_MIN)
            planes = _bit_transpose32([ukey[8 * v:8 * v + 8, :] for v in range(32)])
            grp = pl.multiple_of(off // 32, 8)
            for p in range(32):
                plane_ref[p, pl.ds(grp, 8), :] = planes[p]
        return carry

    for_blocks(idx_body, 0)

    n_words = key_ref.shape[0] // 32
    word_row = lax.broadcasted_iota(jnp.int32, (n_words, tq), 0)
    in_play = jnp.where(word_row < n_rows // 32, jnp.int32(-1), jnp.int32(0))

    def bit_body(p, carry):
        cand, above, ukth = carry
        plane = plane_ref[p]
        ones = cand & plane
        c1 = jnp.sum(lax.population_count(ones).reshape(n_words // 8, 8, tq), axis=0)
        c1 = jnp.sum(c1, axis=0, keepdims=True)
        take = above + c1 >= topk
        cand = jnp.where(take, ones, cand & ~plane)
        above = jnp.where(take, above, above + c1)
        ukth = jnp.where(take, ukth | lax.shift_left(jnp.int32(1), 31 - p), ukth)
        return cand, above, ukth

    zero_row = jnp.zeros((1, tq), jnp.int32)
    _, _, ukth = lax.fori_loop(0, 32, bit_body, (in_play, zero_row, zero_row))
    tau = jnp.maximum(ukth ^ jnp.int32(INT_MIN), jnp.int32(INT_MIN + 1))

    def count32(pred):
        def body(off, size, acc):
            s_idx = off + lax.broadcasted_iota(jnp.int32, (size, 1), 0)
            hit = jnp.where(pred(key_ref[pl.ds(off, size), :], s_idx), 1, 0)
            return acc + jnp.sum(hit.reshape(size // 8, 8, tq), axis=0)

        return jnp.sum(for_blocks(body, jnp.zeros((8, tq), jnp.int32)), axis=0, keepdims=True)

    n_ge = count32(lambda k, s: k >= tau)

    @pl.when(jnp.max(n_ge) > topk)
    def _():
        quota = topk - count32(lambda k, s: k > tau)
        idx_bits = int(key_ref.shape[0]).bit_length()

        def bit_body(bi, cut):
            cand = cut | lax.shift_left(jnp.int32(1), idx_bits - 1 - bi)
            kept = count32(lambda k, s: jnp.where(k == tau, s, cand) < cand)
            return jnp.where(kept <= quota, cand, cut)

        cut = lax.fori_loop(0, idx_bits, bit_body, jnp.zeros((1, tq), jnp.int32))

        def demote(off, size, carry):
            s_idx = off + lax.broadcasted_iota(jnp.int32, (size, 1), 0)
            blk = key_ref[pl.ds(off, size), :]
            key_ref[pl.ds(off, size), :] = jnp.where(jnp.where(blk == tau, s_idx, -1) >= cut, tau - 1, blk)
            return carry

        for_blocks(demote, 0)

    hpg = A_HEADS // A_KV_HEADS
    qgs = [jnp.concatenate([q_ref[:, (g * hpg + h) * LANES:(g * hpg + h + 1) * LANES] for h in range(hpg)], axis=0)
           for g in range(A_KV_HEADS)]
    l_ref[...] = jnp.zeros(l_ref.shape, F32)
    acc_ref[...] = jnp.zeros(acc_ref.shape, F32)

    def masked_scores(off, size, g, sel, fill, shift):
        s = _dot_nt(k_ref[pl.ds(off, size), g * LANES:(g + 1) * LANES], qgs[g])
        return jnp.concatenate([jnp.where(sel, shift(s[:, h * tq:(h + 1) * tq], h), fill) for h in range(hpg)], axis=1)

    def add_pv(off, size, g, p):
        return _dot_tn(v_ref[pl.ds(off, size), g * LANES:(g + 1) * LANES], p.astype(BF16))

    def online():
        m_ref[...] = jnp.full(m_ref.shape, -1e30, F32)

        def body(off, size, carry):
            sel = key_ref[pl.ds(off, size), :] >= tau
            for g in range(A_KV_HEADS):
                s = masked_scores(off, size, g, sel, -jnp.inf, lambda x, h: x)
                m_old = m_ref[g]
                m_new = jnp.maximum(m_old, jnp.max(s, axis=0, keepdims=True))
                alpha = jnp.exp2(m_old - m_new)
                p = jnp.exp2(s - m_new)
                l_ref[g] = alpha * l_ref[g] + jnp.sum(p, axis=0, keepdims=True)
                acc_ref[g] = acc_ref[g] * alpha + add_pv(off, size, g, p)
                m_ref[g] = m_new
            return carry

        for_blocks(body, 0)

    def bounded(bounds):
        def body(off, size, carry):
            sel = key_ref[pl.ds(off, size), :] >= tau
            for g in range(A_KV_HEADS):
                b = bounds[g]
                p = masked_scores(off, size, g, sel, 0.0, lambda x, h: jnp.exp2(x - b[:, h * tq:(h + 1) * tq]))
                l_ref[g] = l_ref[g] + jnp.sum(p, axis=0, keepdims=True)
                acc_ref[g] = acc_ref[g] + add_pv(off, size, g, p)
            return carry

        for_blocks(body, 0)

    if ksq_ref is None:
        online()
    else:
        kmax = jnp.sqrt(jnp.max(ksq_ref[...]))
        ones = jnp.ones((8, LANES), BF16)
        bounds = [jnp.sqrt(_dot_nt(ones, qg * qg)[0:1, :]) * (kmax * BOUND_MARGIN) for qg in qgs]
        safe = jnp.max(jnp.maximum(bounds[0], bounds[1])) <= BOUND_SAFE

        @pl.when(safe)
        def _():
            bounded(bounds)

        @pl.when(jnp.logical_not(safe))
        def _():
            online()

    for g in range(A_KV_HEADS):
        o = acc_ref[g] / l_ref[g]
        for h in range(hpg):
            hh = g * hpg + h
            o_ref[:, hh * LANES:(hh + 1) * LANES] = o[:, h * tq:(h + 1) * tq].T.astype(o_ref.dtype)


def _attn(q, qi, wt, kb, vb, kieo, ksq, tq, causal, n_keys, topk):
    B, T, _ = q.shape
    S = kb.shape[1]
    assert S % ATT_KB[1] == 0 and ATT_KB[0] % ATT_KB[1] == 0 and ATT_KB[1] % IDX_SUB == 0
    hpg = A_HEADS // A_KV_HEADS
    kern = functools.partial(_attn_kernel, tq=tq, kb_sizes=ATT_KB, causal=causal, n_keys=n_keys, topk=topk,
                             has_bound=ksq is not None)
    in_specs = [pl.BlockSpec((None, tq, 1024), lambda b, i: (b, i, 0)),
                pl.BlockSpec((None, tq, 1024), lambda b, i: (b, i, 0)),
                pl.BlockSpec((None, IDX_HEADS, tq), lambda b, i: (b, 0, i)),
                pl.BlockSpec((None, S, 256), lambda b, i: (b, 0, 0)),
                pl.BlockSpec((None, S, 256), lambda b, i: (b, 0, 0)),
                pl.BlockSpec((None, S, 256), lambda b, i: (b, 0, 0))]
    args = [q, qi, wt, kb, vb, kieo]
    if ksq is not None:
        in_specs.append(pl.BlockSpec((None,) + ksq.shape[1:], lambda b, i: (b, 0, 0, 0)))
        args.append(ksq)
    return pl.pallas_call(
        kern,
        out_shape=jax.ShapeDtypeStruct((B, T, 1024), BF16),
        grid=(B, T // tq),
        in_specs=in_specs,
        out_specs=pl.BlockSpec((None, tq, 1024), lambda b, i: (b, i, 0)),
        scratch_shapes=[pltpu.VMEM((S, tq), jnp.int32),
                        pltpu.VMEM((32, S // 32, tq), jnp.int32),
                        pltpu.VMEM((A_KV_HEADS, A_HEAD_DIM, hpg * tq), F32),
                        pltpu.VMEM((A_KV_HEADS, 1, hpg * tq), F32),
                        pltpu.VMEM((A_KV_HEADS, 1, hpg * tq), F32)],
        compiler_params=_cparams(("arbitrary", "arbitrary")),
        name="attn",
    )(*args)


def _gla_consts(C):
    nl = int(np.log2(C))
    M = np.zeros((nl + 2, C, C), np.float32)
    masks = np.zeros((nl + 1, C, C), np.float32)
    for j in range(nl):
        m = 1 << j
        for t in range(C):
            blk = (t // (2 * m)) * 2 * m
            mid = blk + m
            if t >= mid:
                M[j, t, mid:t + 1] = 1
                masks[j, t, blk:mid] = 1
            else:
                M[j, t, t + 1:mid] = 1
    for t in range(C):
        M[nl, t, :t + 1] = 1
        M[nl + 1, t, t + 1:] = 1
    masks[nl] = np.eye(C)
    return M.reshape((nl + 2) * C, C), np.concatenate([masks, masks], axis=1)


def _gla_kernel(q_ref, k_ref, v_ref, lg_ref, gate_ref, mcat_ref, masks_ref, gn_ref, s0_ref,
                o_ref, sout_ref, s_ref, *, tt, C):
    i = pl.program_id(1)
    nl = int(np.log2(C))

    @pl.when(i == 0)
    def _():
        s_ref[...] = s0_ref[...]

    lane = lax.broadcasted_iota(jnp.int32, (1, LANES), 1)
    low = lane < B_KEY_DIM
    row = lax.broadcasted_iota(jnp.int32, (C, 1), 0)
    rr = lax.broadcasted_iota(jnp.int32, (LANES, 2 * B_VAL_DIM), 0)
    cc = lax.broadcasted_iota(jnp.int32, (LANES, 2 * B_VAL_DIM), 1)
    blockdiag = (rr < B_KEY_DIM) == (cc < B_VAL_DIM)
    gn = gn_ref[...]

    def stack2(x):
        return jnp.concatenate([jnp.where(low, x, 0.0), jnp.where(low, 0.0, x)], axis=0).astype(BF16)

    def chunk_body(c, carry):
        r0 = pl.multiple_of(c * C, C)
        qc = q_ref[pl.ds(r0, C), :]
        kc = k_ref[pl.ds(r0, C), :]
        vc = v_ref[pl.ds(r0, C), :]
        lgc = lg_ref[pl.ds(r0, C), :]
        hi = lgc.astype(BF16)
        r1 = lgc - hi.astype(F32)
        mid = r1.astype(BF16)
        lo = (r1 - mid.astype(F32)).astype(BF16)
        mcat = mcat_ref[...]
        E = jnp.exp(_dot(mcat, hi) + _dot(mcat, mid) + _dot(mcat, lo))
        e_b = E[nl * C:(nl + 1) * C, :]
        e_k = E[(nl + 1) * C:(nl + 2) * C, :]
        qt = qc * e_b
        kt = kc * e_k
        e_last = e_b[C - 1:C, :]
        for g in range(B_HEADS // 2):
            sl = slice(g * LANES, (g + 1) * LANES)
            att = _dot_nt(stack2(qc[:, sl]), kc[:, sl].astype(BF16)) * masks_ref[nl]
            for j in range(nl):
                right = ((row >> j) & 1) == 1
                x = jnp.where(right, qc[:, sl], kc[:, sl]) * E[j * C:(j + 1) * C, sl]
                att = att + _dot_nt(stack2(x), x.astype(BF16)) * masks_ref[j]
            att = att.astype(BF16)
            vg = vc[:, g * 2 * B_VAL_DIM:(g + 1) * 2 * B_VAL_DIM]
            sg = s_ref[g]
            o_intra = jnp.concatenate([_dot(att[0:C], vg[:, 0:B_VAL_DIM]),
                                       _dot(att[C:2 * C], vg[:, B_VAL_DIM:2 * B_VAL_DIM])], axis=1)
            o = o_intra + _dot(qt[:, sl].astype(BF16), sg.astype(BF16))
            upd = _dot_tn(kt[:, sl].astype(BF16), vg)
            dec = jnp.broadcast_to(e_last[:, sl], (LANES, LANES)).T
            dec = jnp.concatenate([dec, dec], axis=1)
            s_ref[g] = dec * sg + jnp.where(blockdiag, upd, 0.0)
            for hh in range(2):
                h = 2 * g + hh
                oh = o[:, hh * B_VAL_DIM:(hh + 1) * B_VAL_DIM]
                r = lax.rsqrt(jnp.mean(oh * oh, axis=-1, keepdims=True) + EPS)
                gsl = slice(h * B_VAL_DIM, (h + 1) * B_VAL_DIM)
                o_ref[pl.ds(r0, C), gsl] = (oh * r * gn * gate_ref[pl.ds(r0, C), gsl]).astype(o_ref.dtype)
        return carry

    lax.fori_loop(0, tt // C, chunk_body, 0)

    @pl.when(i == pl.num_programs(1) - 1)
    def _():
        sout_ref[...] = s_ref[...]


def _gla(bq, bk, bv, lg, gate, gn, s0, tt, C):
    B, T, _ = bq.shape
    mcat, masks = _gla_consts(C)
    nl = int(np.log2(C))
    tok = lambda c: pl.BlockSpec((None, tt, c), lambda b, i: (b, i, 0))
    full = lambda shape: pl.BlockSpec(shape, lambda b, i: (0,) * len(shape))
    st = pl.BlockSpec((None, 2, LANES, 2 * B_VAL_DIM), lambda b, i: (b, 0, 0, 0))
    return pl.pallas_call(
        functools.partial(_gla_kernel, tt=tt, C=C),
        out_shape=[jax.ShapeDtypeStruct((B, T, 512), BF16),
                   jax.ShapeDtypeStruct((B, 2, LANES, 2 * B_VAL_DIM), F32)],
        grid=(B, T // tt),
        in_specs=[tok(256), tok(256), tok(512), tok(256), tok(512),
                  full(((nl + 2) * C, C)), full((nl + 1, 2 * C, C)), full((1, B_VAL_DIM)), st],
        out_specs=[tok(512), st],
        scratch_shapes=[pltpu.VMEM((2, LANES, 2 * B_VAL_DIM), F32)],
        compiler_params=_cparams(("arbitrary", "arbitrary")),
        name="gla",
    )(bq, bk, bv, lg, gate, jnp.asarray(mcat, BF16), jnp.asarray(masks, F32), gn, s0)


def _out_kernel(x_ref, oa_ref, ob_ref, oc_ref, wo_ref, g2_ref, h_ref, m_ref):
    acc = _dot(oa_ref[...], wo_ref[0:1024, :])
    acc = acc + _dot(ob_ref[...], wo_ref[1024:1536, :])
    acc = acc + _dot(oc_ref[...], wo_ref[1536:2048, :])
    h = x_ref[...] + acc
    h_ref[...] = h
    ms = jnp.mean(h * h, axis=-1, keepdims=True)
    m_ref[...] = (h * lax.rsqrt(ms + EPS) * g2_ref[...]).astype(BF16)


def _out_proj(x, oa, ob, oc, wo, g2, bm):
    M, D = x.shape
    row = lambda c: pl.BlockSpec((bm, c), lambda i: (i, 0))
    return pl.pallas_call(
        _out_kernel,
        out_shape=[jax.ShapeDtypeStruct((M, D), F32), jax.ShapeDtypeStruct((M, D), BF16)],
        grid=(M // bm,),
        in_specs=[row(D), row(1024), row(512), row(512),
                  pl.BlockSpec((D, D), lambda i: (0, 0)), pl.BlockSpec((1, D), lambda i: (0, 0))],
        out_specs=[row(D), row(D)],
        compiler_params=_cparams(("arbitrary",)),
        name="out_proj",
    )(x, oa, ob, oc, wo, g2.reshape(1, D))


def _ffn_kernel(h_ref, m_ref, wg_ref, wu_ref, wd_ref, y_ref):
    @pl.when(pl.program_id(1) == 0)
    def _():
        y_ref[...] = h_ref[...]

    m = m_ref[...]
    gate = _dot(m, wg_ref[...])
    up = _dot(m, wu_ref[...])
    a = (gate * (1.0 / (1.0 + jnp.exp(-gate))) * up).astype(BF16)
    y_ref[...] += _dot(a, wd_ref[...])


def _ffn(h, m, wg, wu, wd, bm, bf):
    M, D = h.shape
    Fh = wg.shape[1]
    return pl.pallas_call(
        _ffn_kernel,
        out_shape=jax.ShapeDtypeStruct((M, D), F32),
        grid=(M // bm, Fh // bf),
        in_specs=[pl.BlockSpec((bm, D), lambda i, j: (i, 0)),
                  pl.BlockSpec((bm, D), lambda i, j: (i, 0)),
                  pl.BlockSpec((D, bf), lambda i, j: (0, j)),
                  pl.BlockSpec((D, bf), lambda i, j: (0, j)),
                  pl.BlockSpec((bf, D), lambda i, j: (j, 0))],
        out_specs=pl.BlockSpec((bm, D), lambda i, j: (i, 0)),
        compiler_params=_cparams(("arbitrary", "arbitrary")),
        name="ffn",
    )(h, m, wg, wu, wd)


def _rope_tables(pos0, T):
    pos = (pos0 + jnp.arange(T, dtype=jnp.int32)).astype(F32)

    def cs(half):
        inv = ROPE_THETA ** (-jnp.arange(half, dtype=F32) / half)
        ang = pos[:, None] * inv[None, :]
        return jnp.cos(ang), jnp.sin(ang)

    c64, s64 = cs(A_HEAD_DIM // 2)
    c32, s32 = cs(IDX_DIM // 2)
    z32 = jnp.zeros_like(s32)
    return (jnp.concatenate([c64, c64], axis=1), jnp.concatenate([-s64, s64], axis=1),
            jnp.concatenate([c32, c32, c32, c32], axis=1),
            jnp.concatenate([-s32, z32, -s32, z32], axis=1),
            jnp.concatenate([z32, s32, z32, s32], axis=1))


def _pack_w_in(w_in):
    cols = []
    for name in _PACK_ORDER:
        so, w, _, pw = _LAYOUT[name]
        piece = w_in[:, :, so:so + w]
        if pw > w:
            piece = jnp.pad(piece, ((0, 0), (0, 0), (0, pw - w)))
        cols.append(piece)
    return jnp.concatenate(cols, axis=-1).astype(BF16)


def _state_to_groups(s):
    B = s.shape[0]
    s = s.reshape(B, 2, 2, B_KEY_DIM, B_VAL_DIM)
    z = jnp.zeros_like(s[:, :, 0])
    top = jnp.concatenate([s[:, :, 0], z], axis=-1)
    bot = jnp.concatenate([z, s[:, :, 1]], axis=-1)
    return jnp.concatenate([top, bot], axis=-2)


def _groups_to_state(sg):
    B = sg.shape[0]
    h0 = sg[:, :, :B_KEY_DIM, :B_VAL_DIM]
    h1 = sg[:, :, B_KEY_DIM:, B_VAL_DIM:]
    return jnp.stack([h0, h1], axis=2).reshape(B, B_HEADS, B_KEY_DIM, B_VAL_DIM)


def _layer(x, pos0, cache, prm):
    (g1, w_in_p, gq, gk, gki, w_alpha, b_alpha, g_gla, pool_w, pool_scale, w_o, g2, w_gate, w_up, w_down) = prm
    B, T, D = x.shape
    M = B * T
    x2 = x.reshape(M, D)
    bm = min(512, M)
    P = _norm_matmul(x2, g1, w_in_p, min(1024, M), PACKED_WIDTH // 3).reshape(B, T, PACKED_WIDTH)

    tt = min(512, T)
    tabs = _rope_tables(pos0, T)
    gki_p = jnp.pad(gki, (0, LANES - IDX_DIM)).reshape(1, LANES)
    w_alpha_p = jnp.pad(w_alpha, ((0, LANES - B_GATE_RANK), (0, 0))).astype(BF16)
    if cache is None:
        hist0 = jnp.zeros((B, HIST_ROWS, C_WIDTH), F32)
    else:
        hist0 = jnp.pad(cache[4], ((0, 0), (HIST_ROWS - POOL_HIST, 0), (0, 0)))
    (q, qi, k32, kb, vb, ki32, kieo, wt, bq, bk, bv, lg, gate, oc, ksq) = _post(
        P, tabs, gq.reshape(1, LANES), gk.reshape(1, LANES), gki_p, w_alpha_p, b_alpha.reshape(1, -1),
        pool_w.astype(BF16), pool_scale.reshape(1, -1), hist0, tt, pos0)

    v_off, _ = _seg("a_v")
    v32 = P[:, :, v_off:v_off + 256]
    u_off, _ = _seg("c_u")
    pool_state = P[:, T - POOL_HIST:, u_off:u_off + C_WIDTH]

    if cache is None:
        tq = LANES
        oa = _attn(q, qi, wt, kb, vb, kieo, ksq, tq, True, T, min(TOPK_MAX, T // 4))
        s0 = jnp.zeros((B, 2, LANES, 2 * B_VAL_DIM), F32)
        ob, sg = _gla(bq, bk, bv, lg, gate, g_gla.reshape(1, -1), s0, tt, CHUNK)
    else:
        ck, cv, cki, c_gla, _ = cache
        past = ck.shape[1]
        L = past + T
        S = -(-L // ATT_KB[1]) * ATT_KB[1]
        padk = lambda a: jnp.pad(a, ((0, 0), (0, S - L), (0, 0)))
        k_all = padk(jnp.concatenate([ck.reshape(B, past, 256).astype(BF16), kb], axis=1))
        v_all = padk(jnp.concatenate([cv.reshape(B, past, 256).astype(BF16), vb], axis=1))
        zc = jnp.zeros((B, past, IDX_DIM), BF16)
        cki_b = cki.astype(BF16)
        ki_all = padk(jnp.concatenate([jnp.concatenate([cki_b, zc, zc, cki_b], axis=-1), kieo], axis=1))
        tq = LANES
        padq = lambda a: jnp.pad(a, ((0, 0), (0, tq - T), (0, 0)))
        wt_p = jnp.pad(wt, ((0, 0), (0, 0), (0, tq - T)))
        oa = _attn(padq(q), padq(qi), wt_p, k_all, v_all, ki_all, None, tq, False, L,
                   min(TOPK_MAX, L // 4))[:, :T]
        ob, sg = _gla(bq, bk, bv, lg, gate, g_gla.reshape(1, -1), _state_to_groups(c_gla), tt, T)

    h, m = _out_proj(x2, oa.reshape(M, -1), ob.reshape(M, -1), oc.reshape(M, -1), w_o, g2, bm)
    y = _ffn(h, m, w_gate, w_up, w_down, min(1024, M), 512)
    state = (k32.reshape(B, T, A_KV_HEADS, A_HEAD_DIM), v32.reshape(B, T, A_KV_HEADS, A_HEAD_DIM),
             ki32, _groups_to_state(sg), pool_state)
    return y.reshape(B, T, D), state


def kernel(x_prompt, x_sample, cache_k, cache_v, cache_kidx, state_gla, state_pool,
           norm1, w_in, q_norm, k_norm, kidx_norm, w_alpha, b_alpha, gla_norm, pool_w, pool_scale,
           w_o, norm2, w_gate, w_up, w_down):
    depth = w_in.shape[0]
    past = cache_k.shape[2]
    w_in_p = _pack_w_in(w_in)
    w_o_b, w_gate_b, w_up_b, w_down_b = (w.astype(BF16) for w in (w_o, w_gate, w_up, w_down))
    yp, ys = x_prompt, x_sample
    st_p, st_s = [], []
    for l in range(depth):
        prm = (norm1[l], w_in_p[l], q_norm[l], k_norm[l], kidx_norm[l], w_alpha[l], b_alpha[l], gla_norm[l],
               pool_w[l], pool_scale[l], w_o_b[l], norm2[l], w_gate_b[l], w_up_b[l], w_down_b[l])
        yp, sp = _layer(yp, 0, None, prm)
        ys, ss = _layer(ys, past, (cache_k[l], cache_v[l], cache_kidx[l], state_gla[l], state_pool[l]), prm)
        st_p.append(sp)
        st_s.append(ss)
    stk = lambda sts, i: jnp.stack([s[i] for s in sts], axis=0)
    return (yp, ys,
            stk(st_p, 0), stk(st_p, 1), stk(st_p, 2), stk(st_p, 3), stk(st_p, 4),
            stk(st_s, 0), stk(st_s, 1), stk(st_s, 2), stk(st_s, 3), stk(st_s, 4))
```

```python
import functools

import numpy as np
import jax
import jax.numpy as jnp
from jax import lax
from jax.experimental import pallas as pl
from jax.experimental.pallas import tpu as pltpu

F32 = jnp.float32
BF16 = jnp.bfloat16

EPS = 1e-6
CHUNK = 64
TOPK_MAX = 256
ROPE_THETA = 10000.0
A_HEAD_DIM = 128
A_HEADS = 8
A_KV_HEADS = 2
IDX_HEADS = 16
IDX_DIM = 64
IDX_W_SCALE = (IDX_HEADS ** -0.5) * (IDX_DIM ** -0.5)
B_HEADS = 4
B_KEY_DIM = 64
B_VAL_DIM = 128
B_GATE_RANK = 16
B_GATE_TEMP = 16.0
C_WINDOWS = (2, 4, 8, 16)
C_WIDTH = 512
C_GROUP = 128
POOL_HIST = 15
HIST_ROWS = 16

LANES = 128
VMEM_LIMIT = 60 * 1024 * 1024

INT_MIN = -(2 ** 31)
IDX_SUB = 256
ATT_KB = (2048, 512)
Q_SCALE = (A_HEAD_DIM ** -0.5) * float(np.log2(np.e))
BOUND_MARGIN = 1.02
BOUND_SAFE = 60.0

_SRC_SIZES = (("a_q", 1024), ("a_k", 256), ("a_v", 256), ("a_qi", 1024), ("a_ki", 64), ("a_wi", 16),
              ("b_q", 256), ("b_k", 256), ("b_v", 512), ("b_a", 16), ("b_g", 512), ("c_u", 512))
_PACK_ORDER = ("a_q", "a_qi", "b_v", "b_g", "c_u", "a_k", "a_v", "b_q", "b_k", "a_ki", "a_wi", "b_a")


def _build_layout():
    src, off = {}, 0
    for name, w in _SRC_SIZES:
        src[name] = (off, w)
        off += w
    lay, poff = {}, 0
    for name in _PACK_ORDER:
        so, w = src[name]
        pw = -(-w // LANES) * LANES
        lay[name] = (so, w, poff, pw)
        poff += pw
    return lay, poff


_LAYOUT, PACKED_WIDTH = _build_layout()


def _seg(name):
    _, _, poff, pw = _LAYOUT[name]
    return poff, pw


def _dot(a, b):
    return jnp.dot(a, b, preferred_element_type=F32)


def _dot_nt(a, b):
    return lax.dot_general(a, b, (((1,), (1,)), ((), ())), preferred_element_type=F32)


def _dot_tn(a, b):
    return lax.dot_general(a, b, (((0,), (0,)), ((), ())), preferred_element_type=F32)


def _bit_transpose32(rows):
    a = list(rows)
    j, m = 16, 0x0000FFFF
    while j:
        k = 0
        while k < 32:
            t = (a[k] ^ lax.shift_right_logical(a[k + j], jnp.int32(j))) & jnp.int32(m)
            a[k] = a[k] ^ t
            a[k + j] = a[k + j] ^ lax.shift_left(t, jnp.int32(j))
            k = (k + j + 1) & ~j
        j >>= 1
        m = (m ^ (m << j)) & 0xFFFFFFFF
    return a


def _cparams(sem):
    return pltpu.CompilerParams(dimension_semantics=sem, vmem_limit_bytes=VMEM_LIMIT)


def _norm_matmul_kernel(x_ref, g_ref, w_ref, o_ref, xn_ref):
    @pl.when(pl.program_id(1) == 0)
    def _():
        x = x_ref[...]
        ms = jnp.mean(x * x, axis=-1, keepdims=True)
        xn_ref[...] = (x * lax.rsqrt(ms + EPS) * g_ref[...]).astype(BF16)

    o_ref[...] = _dot(xn_ref[...], w_ref[...])


def _norm_matmul(x, g, w, bm, bn):
    M, D = x.shape
    N = w.shape[1]
    return pl.pallas_call(
        _norm_matmul_kernel,
        out_shape=jax.ShapeDtypeStruct((M, N), F32),
        grid=(M // bm, N // bn),
        in_specs=[pl.BlockSpec((bm, D), lambda i, j: (i, 0)),
                  pl.BlockSpec((1, D), lambda i, j: (0, 0)),
                  pl.BlockSpec((D, bn), lambda i, j: (0, j))],
        out_specs=pl.BlockSpec((bm, bn), lambda i, j: (i, j)),
        scratch_shapes=[pltpu.VMEM((bm, D), BF16)],
        compiler_params=_cparams(("arbitrary", "arbitrary")),
        name="norm_matmul",
    )(x, g.reshape(1, D), w)


def _post_kernel(p_ref, cosa_ref, sina_ref, cosb_ref, sinlo_ref, sinhi_ref, gq_ref, gk_ref, gki_ref,
                 walpha_ref, balpha_ref, poolw_ref, pscale_ref, hist0_ref,
                 q_ref, qi_ref, k32_ref, kb_ref, vb_ref, ki32_ref, kieo_ref, wt_ref,
                 bq_ref, bk_ref, bv_ref, lg_ref, gate_ref, oc_ref, ksq_ref,
                 uext_ref, hist_ref, *, tt, pos0):
    i = pl.program_id(1)

    def seg(name):
        off, w = _seg(name)
        return p_ref[:, off:off + w]

    cosa, sina = cosa_ref[...], sina_ref[...]

    def head_norm_rope(a, g):
        r = lax.rsqrt(jnp.mean(a * a, axis=-1, keepdims=True) + EPS)
        y = a * r * g
        return y * cosa + pltpu.roll(y, 64, 1) * sina

    aq = seg("a_q")
    gq = gq_ref[...]
    for h in range(A_HEADS):
        sl = slice(h * LANES, (h + 1) * LANES)
        q_ref[:, sl] = (head_norm_rope(aq[:, sl], gq) * Q_SCALE).astype(BF16)
    ak = seg("a_k")
    gk = gk_ref[...]
    ksq = jnp.zeros((tt, 1), F32)
    for h in range(A_KV_HEADS):
        sl = slice(h * LANES, (h + 1) * LANES)
        kk = head_norm_rope(ak[:, sl], gk)
        k32_ref[:, sl] = kk
        kb_ref[:, sl] = kk.astype(BF16)
        ksq = jnp.maximum(ksq, jnp.sum(kk * kk, axis=-1, keepdims=True))
    ksq_ref[...] = jnp.broadcast_to(jnp.max(ksq, axis=0, keepdims=True), ksq_ref.shape)
    vb_ref[...] = seg("a_v").astype(BF16)

    cosb, sinlo, sinhi = cosb_ref[...], sinlo_ref[...], sinhi_ref[...]

    def rope64(y):
        return y * cosb + pltpu.roll(y, 96, 1) * sinlo + pltpu.roll(y, 32, 1) * sinhi

    aqi = seg("a_qi")
    for gidx in range(IDX_HEADS // 2):
        sl = slice(gidx * LANES, (gidx + 1) * LANES)
        qi_ref[:, sl] = rope64(aqi[:, sl]).astype(BF16)
    aki = seg("a_ki")
    rki = lax.rsqrt(jnp.sum(aki * aki, axis=-1, keepdims=True) * (1.0 / IDX_DIM) + EPS)
    ki = rope64(aki * rki * gki_ref[...])
    ki32_ref[...] = ki[:, :IDX_DIM]
    kieo_ref[:, 0:LANES] = ki.astype(BF16)
    kieo_ref[:, LANES:2 * LANES] = pltpu.roll(ki, 64, 1).astype(BF16)

    awi = seg("a_wi") * IDX_W_SCALE
    if tt % LANES == 0:
        for c in range(tt // LANES):
            wt_ref[:, c * LANES:(c + 1) * LANES] = awi[c * LANES:(c + 1) * LANES, :].T[:IDX_HEADS, :]
    else:
        padded = jnp.concatenate([awi, jnp.zeros((LANES - tt, LANES), F32)], axis=0)
        wt_ref[...] = padded.T[:IDX_HEADS, :tt]

    bq_ref[...] = seg("b_q") * (B_KEY_DIM ** -0.5)
    bk_ref[...] = seg("b_k")
    bv_ref[...] = seg("b_v").astype(BF16)
    z = _dot(seg("b_a").astype(BF16), walpha_ref[...]) + balpha_ref[...]
    lg_ref[...] = (jnp.minimum(z, 0.0) - jnp.log1p(jnp.exp(-jnp.abs(z)))) * (1.0 / B_GATE_TEMP)
    bg = seg("b_g")
    gate_ref[...] = bg * (1.0 / (1.0 + jnp.exp(-bg)))

    @pl.when(i == 0)
    def _():
        hist_ref[...] = hist0_ref[...]

    u = seg("c_u")
    uext_ref[0:HIST_ROWS, :] = hist_ref[...]
    uext_ref[HIST_ROWS:HIST_ROWS + tt, :] = u
    hist_ref[...] = uext_ref[tt:tt + HIST_ROWS, :]
    pos = pos0 + i * tt + lax.broadcasted_iota(jnp.int32, (tt, 1), 0)
    for gi, w in enumerate(C_WINDOWS):
        sl = slice(gi * C_GROUP, (gi + 1) * C_GROUP)
        s = uext_ref[:, sl]
        shift = 1
        while shift < w:
            s = s + pltpu.roll(s, shift, 0)
            shift *= 2
        cnt = jnp.minimum(w, pos + 1).astype(F32)
        d = s[HIST_ROWS:, :] / cnt - u[:, sl]
        y = _dot(d.astype(BF16), poolw_ref[gi]) * pscale_ref[:, sl]
        oc_ref[:, sl] = y.astype(BF16)


def _post(P, tabs, gq, gk, gki, w_alpha, b_alpha, pool_w, pool_scale, hist0, tt, pos0):
    B, T, W = P.shape
    nT = T // tt
    tok = lambda c: pl.BlockSpec((None, tt, c), lambda b, i: (b, i, 0))
    tab = pl.BlockSpec((tt, LANES), lambda b, i: (i, 0))
    full = lambda shape: pl.BlockSpec(shape, lambda b, i: (0,) * len(shape))
    out_shape = [
        jax.ShapeDtypeStruct((B, T, 1024), BF16),
        jax.ShapeDtypeStruct((B, T, 1024), BF16),
        jax.ShapeDtypeStruct((B, T, 256), F32),
        jax.ShapeDtypeStruct((B, T, 256), BF16),
        jax.ShapeDtypeStruct((B, T, 256), BF16),
        jax.ShapeDtypeStruct((B, T, IDX_DIM), F32),
        jax.ShapeDtypeStruct((B, T, 256), BF16),
        jax.ShapeDtypeStruct((B, IDX_HEADS, T), F32),
        jax.ShapeDtypeStruct((B, T, 256), F32),
        jax.ShapeDtypeStruct((B, T, 256), F32),
        jax.ShapeDtypeStruct((B, T, 512), BF16),
        jax.ShapeDtypeStruct((B, T, 256), F32),
        jax.ShapeDtypeStruct((B, T, 512), F32),
        jax.ShapeDtypeStruct((B, T, 512), BF16),
        jax.ShapeDtypeStruct((B, nT, 8, LANES), F32),
    ]
    out_specs = [tok(1024), tok(1024), tok(256), tok(256), tok(256), tok(IDX_DIM), tok(256),
                 pl.BlockSpec((None, IDX_HEADS, tt), lambda b, i: (b, 0, i)),
                 tok(256), tok(256), tok(512), tok(256), tok(512), tok(512),
                 pl.BlockSpec((None, None, 8, LANES), lambda b, i: (b, i, 0, 0))]
    return pl.pallas_call(
        functools.partial(_post_kernel, tt=tt, pos0=pos0),
        out_shape=out_shape,
        grid=(B, nT),
        in_specs=[tok(W), tab, tab, tab, tab, tab,
                  full((1, LANES)), full((1, LANES)), full((1, LANES)),
                  full((LANES, 256)), full((1, 256)),
                  full((len(C_WINDOWS), C_GROUP, C_GROUP)), full((1, C_WIDTH)),
                  pl.BlockSpec((None, HIST_ROWS, C_WIDTH), lambda b, i: (b, 0, 0))],
        out_specs=out_specs,
        scratch_shapes=[pltpu.VMEM((tt + HIST_ROWS, C_WIDTH), F32), pltpu.VMEM((HIST_ROWS, C_WIDTH), F32)],
        compiler_params=_cparams(("arbitrary", "arbitrary")),
        name="post",
    )(P, *tabs, gq, gk, gki, w_alpha, b_alpha, pool_w, pool_scale, hist0)


def _attn_kernel(*refs, tq, kb_sizes, causal, n_keys, topk, has_bound):
    q_ref, qi_ref, wt_ref, k_ref, v_ref, ki_ref = refs[:6]
    ksq_ref = refs[6] if has_bound else None
    o_ref, key_ref, plane_ref, acc_ref, m_ref, l_ref = refs[-6:]
    i = pl.program_id(1)

    @pl.when(i == 0)
    def _():
        plane_ref[...] = jnp.zeros(plane_ref.shape, jnp.int32)

    n_slabs = IDX_HEADS // 2
    kb_big, kb_small = kb_sizes
    if causal:
        n_vis = (i + 1) * tq
        t_abs = i * tq + lax.broadcasted_iota(jnp.int32, (1, tq), 1)
        limit = (t_abs // CHUNK + 1) * CHUNK
    else:
        n_vis = n_keys
        limit = jnp.full((1, tq), n_keys, jnp.int32)
    n_big = (n_vis - 1) // kb_big
    n_small = (n_vis - n_big * kb_big + kb_small - 1) // kb_small
    n_rows = n_big * kb_big + n_small * kb_small

    def for_blocks(body, carry):
        def run(size, first, count, c):
            return lax.fori_loop(0, count, lambda j, cc: body(pl.multiple_of(first + j * size, size), size, cc), c)

        return run(kb_small, n_big * kb_big, n_small, run(kb_big, 0, n_big, carry))

    q2 = jnp.concatenate([qi_ref[:, g * LANES:(g + 1) * LANES] for g in range(n_slabs)], axis=0)

    def idx_body(off0, size, carry):
        for sb in range(size // IDX_SUB):
            off = pl.multiple_of(off0 + sb * IDX_SUB, IDX_SUB)
            ye = _dot_nt(ki_ref[pl.ds(off, IDX_SUB), 0:LANES], q2)
            yo = _dot_nt(ki_ref[pl.ds(off, IDX_SUB), LANES:2 * LANES], q2)
            score = jnp.zeros((IDX_SUB, tq), F32)
            for g in range(n_slabs):
                score = score + jnp.maximum(ye[:, g * tq:(g + 1) * tq], 0.0) * wt_ref[2 * g:2 * g + 1, :]
                score = score + jnp.maximum(yo[:, g * tq:(g + 1) * tq], 0.0) * wt_ref[2 * g + 1:2 * g + 2, :]
            bits = pltpu.bitcast(score, jnp.int32)
            key = bits ^ ((bits >> 31) & jnp.int32(0x7FFFFFFF))
            if size == kb_small:
                s_idx = off + lax.broadcasted_iota(jnp.int32, (IDX_SUB, 1), 0)
                key = jnp.where(s_idx < limit, key, jnp.int32(INT_MIN))
            key_ref[pl.ds(off, IDX_SUB), :] = key
            ukey = key ^ jnp.int32(INT_MIN)
            planes = _bit_transpose32([ukey[8 * v:8 * v + 8, :] for v in range(32)])
            grp = pl.multiple_of(off // 32, 8)
            for p in range(32):
                plane_ref[p, pl.ds(grp, 8), :] = planes[p]
        return carry

    for_blocks(idx_body, 0)

    n_words = key_ref.shape[0] // 32
    word_row = lax.broadcasted_iota(jnp.int32, (n_words, tq), 0)
    in_play = jnp.where(word_row < n_rows // 32, jnp.int32(-1), jnp.int32(0))

    def bit_body(p, carry):
        cand, above, ukth = carry
        plane = plane_ref[p]
        ones = cand & plane
        c1 = jnp.sum(lax.population_count(ones).reshape(n_words // 8, 8, tq), axis=0)
        c1 = jnp.sum(c1, axis=0, keepdims=True)
        take = above + c1 >= topk
        cand = jnp.where(take, ones, cand & ~plane)
        above = jnp.where(take, above, above + c1)
        ukth = jnp.where(take, ukth | lax.shift_left(jnp.int32(1), 31 - p), ukth)
        return cand, above, ukth

    zero_row = jnp.zeros((1, tq), jnp.int32)
    at_kth, above, ukth = lax.fori_loop(0, 32, bit_body, (in_play, zero_row, zero_row))
    tau = jnp.maximum(ukth ^ jnp.int32(INT_MIN), jnp.int32(INT_MIN + 1))

    def count32(pred):
        def body(off, size, acc):
            s_idx = off + lax.broadcasted_iota(jnp.int32, (size, 1), 0)
            hit = jnp.where(pred(key_ref[pl.ds(off, size), :], s_idx), 1, 0)
            return acc + jnp.sum(hit.reshape(size // 8, 8, tq), axis=0)

        return jnp.sum(for_blocks(body, jnp.zeros((8, tq), jnp.int32)), axis=0, keepdims=True)

    n_at = jnp.sum(lax.population_count(at_kth).reshape(n_words // 8, 8, tq), axis=0)
    n_ge = jnp.where(ukth != 0, above + jnp.sum(n_at, axis=0, keepdims=True), 0)

    @pl.when(jnp.max(n_ge) > topk)
    def _():
        quota = topk - count32(lambda k, s: k > tau)
        idx_bits = int(key_ref.shape[0]).bit_length()

        def bit_body(bi, cut):
            cand = cut | lax.shift_left(jnp.int32(1), idx_bits - 1 - bi)
            kept = count32(lambda k, s: jnp.where(k == tau, s, cand) < cand)
            return jnp.where(kept <= quota, cand, cut)

        cut = lax.fori_loop(0, idx_bits, bit_body, jnp.zeros((1, tq), jnp.int32))

        def demote(off, size, carry):
            s_idx = off + lax.broadcasted_iota(jnp.int32, (size, 1), 0)
            blk = key_ref[pl.ds(off, size), :]
            key_ref[pl.ds(off, size), :] = jnp.where(jnp.where(blk == tau, s_idx, -1) >= cut, tau - 1, blk)
            return carry

        for_blocks(demote, 0)

    hpg = A_HEADS // A_KV_HEADS
    qgs = [jnp.concatenate([q_ref[:, (g * hpg + h) * LANES:(g * hpg + h + 1) * LANES] for h in range(hpg)], axis=0)
           for g in range(A_KV_HEADS)]
    l_ref[...] = jnp.zeros(l_ref.shape, F32)
    acc_ref[...] = jnp.zeros(acc_ref.shape, F32)

    def masked_scores(off, size, g, sel, fill, shift):
        s = _dot_nt(k_ref[pl.ds(off, size), g * LANES:(g + 1) * LANES], qgs[g])
        return jnp.concatenate([jnp.where(sel, shift(s[:, h * tq:(h + 1) * tq], h), fill) for h in range(hpg)], axis=1)

    def add_pv(off, size, g, p):
        return _dot_tn(v_ref[pl.ds(off, size), g * LANES:(g + 1) * LANES], p.astype(BF16))

    def online():
        m_ref[...] = jnp.full(m_ref.shape, -1e30, F32)

        def body(off, size, carry):
            sel = key_ref[pl.ds(off, size), :] >= tau
            for g in range(A_KV_HEADS):
                s = masked_scores(off, size, g, sel, -jnp.inf, lambda x, h: x)
                m_old = m_ref[g]
                m_new = jnp.maximum(m_old, jnp.max(s, axis=0, keepdims=True))
                alpha = jnp.exp2(m_old - m_new)
                p = jnp.exp2(s - m_new)
                l_ref[g] = alpha * l_ref[g] + jnp.sum(p, axis=0, keepdims=True)
                acc_ref[g] = acc_ref[g] * alpha + add_pv(off, size, g, p)
                m_ref[g] = m_new
            return carry

        for_blocks(body, 0)

    def bounded(bounds):
        def body(off, size, carry):
            sel = key_ref[pl.ds(off, size), :] >= tau
            for g in range(A_KV_HEADS):
                b = bounds[g]
                p = masked_scores(off, size, g, sel, 0.0, lambda x, h: jnp.exp2(x - b[:, h * tq:(h + 1) * tq]))
                l_ref[g] = l_ref[g] + jnp.sum(p, axis=0, keepdims=True)
                acc_ref[g] = acc_ref[g] + add_pv(off, size, g, p)
            return carry

        for_blocks(body, 0)

    if ksq_ref is None:
        online()
    else:
        kmax = jnp.sqrt(jnp.max(ksq_ref[...]))
        ones = jnp.ones((8, LANES), BF16)
        bounds = [jnp.sqrt(_dot_nt(ones, qg * qg)[0:1, :]) * (kmax * BOUND_MARGIN) for qg in qgs]
        safe = jnp.max(jnp.maximum(bounds[0], bounds[1])) <= BOUND_SAFE

        @pl.when(safe)
        def _():
            bounded(bounds)

        @pl.when(jnp.logical_not(safe))
        def _():
            online()

    for g in range(A_KV_HEADS):
        o = acc_ref[g] / l_ref[g]
        for h in range(hpg):
            hh = g * hpg + h
            o_ref[:, hh * LANES:(hh + 1) * LANES] = o[:, h * tq:(h + 1) * tq].T.astype(o_ref.dtype)


def _attn(q, qi, wt, kb, vb, kieo, ksq, tq, causal, n_keys, topk):
    B, T, _ = q.shape
    S = kb.shape[1]
    assert S % ATT_KB[1] == 0 and ATT_KB[0] % ATT_KB[1] == 0 and ATT_KB[1] % IDX_SUB == 0
    hpg = A_HEADS // A_KV_HEADS
    kern = functools.partial(_attn_kernel, tq=tq, kb_sizes=ATT_KB, causal=causal, n_keys=n_keys, topk=topk,
                             has_bound=ksq is not None)
    in_specs = [pl.BlockSpec((None, tq, 1024), lambda b, i: (b, i, 0)),
                pl.BlockSpec((None, tq, 1024), lambda b, i: (b, i, 0)),
                pl.BlockSpec((None, IDX_HEADS, tq), lambda b, i: (b, 0, i)),
                pl.BlockSpec((None, S, 256), lambda b, i: (b, 0, 0)),
                pl.BlockSpec((None, S, 256), lambda b, i: (b, 0, 0)),
                pl.BlockSpec((None, S, 256), lambda b, i: (b, 0, 0))]
    args = [q, qi, wt, kb, vb, kieo]
    if ksq is not None:
        in_specs.append(pl.BlockSpec((None,) + ksq.shape[1:], lambda b, i: (b, 0, 0, 0)))
        args.append(ksq)
    return pl.pallas_call(
        kern,
        out_shape=jax.ShapeDtypeStruct((B, T, 1024), BF16),
        grid=(B, T // tq),
        in_specs=in_specs,
        out_specs=pl.BlockSpec((None, tq, 1024), lambda b, i: (b, i, 0)),
        scratch_shapes=[pltpu.VMEM((S, tq), jnp.int32),
                        pltpu.VMEM((32, S // 32, tq), jnp.int32),
                        pltpu.VMEM((A_KV_HEADS, A_HEAD_DIM, hpg * tq), F32),
                        pltpu.VMEM((A_KV_HEADS, 1, hpg * tq), F32),
                        pltpu.VMEM((A_KV_HEADS, 1, hpg * tq), F32)],
        compiler_params=_cparams(("arbitrary", "arbitrary")),
        name="attn",
    )(*args)


def _gla_consts(C):
    nl = int(np.log2(C))
    M = np.zeros((nl + 2, C, C), np.float32)
    masks = np.zeros((nl + 1, C, C), np.float32)
    for j in range(nl):
        m = 1 << j
        for t in range(C):
            blk = (t // (2 * m)) * 2 * m
            mid = blk + m
            if t >= mid:
                M[j, t, mid:t + 1] = 1
                masks[j, t, blk:mid] = 1
            else:
                M[j, t, t + 1:mid] = 1
    for t in range(C):
        M[nl, t, :t + 1] = 1
        M[nl + 1, t, t + 1:] = 1
    masks[nl] = np.eye(C)
    return M.reshape((nl + 2) * C, C), np.concatenate([masks, masks], axis=1)


def _gla_kernel(q_ref, k_ref, v_ref, lg_ref, gate_ref, mcat_ref, masks_ref, gn_ref, s0_ref,
                o_ref, sout_ref, s_ref, *, tt, C):
    i = pl.program_id(1)
    nl = int(np.log2(C))

    @pl.when(i == 0)
    def _():
        s_ref[...] = s0_ref[...]

    lane = lax.broadcasted_iota(jnp.int32, (1, LANES), 1)
    low = lane < B_KEY_DIM
    row = lax.broadcasted_iota(jnp.int32, (C, 1), 0)
    rr = lax.broadcasted_iota(jnp.int32, (LANES, 2 * B_VAL_DIM), 0)
    cc = lax.broadcasted_iota(jnp.int32, (LANES, 2 * B_VAL_DIM), 1)
    blockdiag = (rr < B_KEY_DIM) == (cc < B_VAL_DIM)
    gn = gn_ref[...]

    def stack2(x):
        return jnp.concatenate([jnp.where(low, x, 0.0), jnp.where(low, 0.0, x)], axis=0).astype(BF16)

    def chunk_body(c, carry):
        r0 = pl.multiple_of(c * C, C)
        qc = q_ref[pl.ds(r0, C), :]
        kc = k_ref[pl.ds(r0, C), :]
        vc = v_ref[pl.ds(r0, C), :]
        lgc = lg_ref[pl.ds(r0, C), :]
        hi = lgc.astype(BF16)
        r1 = lgc - hi.astype(F32)
        mid = r1.astype(BF16)
        lo = (r1 - mid.astype(F32)).astype(BF16)
        mcat = mcat_ref[...]
        E = jnp.exp(_dot(mcat, hi) + _dot(mcat, mid) + _dot(mcat, lo))
        e_b = E[nl * C:(nl + 1) * C, :]
        e_k = E[(nl + 1) * C:(nl + 2) * C, :]
        qt = qc * e_b
        kt = kc * e_k
        e_last = e_b[C - 1:C, :]
        for g in range(B_HEADS // 2):
            sl = slice(g * LANES, (g + 1) * LANES)
            att = _dot_nt(stack2(qc[:, sl]), kc[:, sl].astype(BF16)) * masks_ref[nl]
            for j in range(nl):
                right = ((row >> j) & 1) == 1
                x = jnp.where(right, qc[:, sl], kc[:, sl]) * E[j * C:(j + 1) * C, sl]
                att = att + _dot_nt(stack2(x), x.astype(BF16)) * masks_ref[j]
            att = att.astype(BF16)
            vg = vc[:, g * 2 * B_VAL_DIM:(g + 1) * 2 * B_VAL_DIM]
            sg = s_ref[g]
            o_intra = jnp.concatenate([_dot(att[0:C], vg[:, 0:B_VAL_DIM]),
                                       _dot(att[C:2 * C], vg[:, B_VAL_DIM:2 * B_VAL_DIM])], axis=1)
            o = o_intra + _dot(qt[:, sl].astype(BF16), sg.astype(BF16))
            upd = _dot_tn(kt[:, sl].astype(BF16), vg)
            dec = jnp.broadcast_to(e_last[:, sl], (LANES, LANES)).T
            dec = jnp.concatenate([dec, dec], axis=1)
            s_ref[g] = dec * sg + jnp.where(blockdiag, upd, 0.0)
            for hh in range(2):
                h = 2 * g + hh
                oh = o[:, hh * B_VAL_DIM:(hh + 1) * B_VAL_DIM]
                r = lax.rsqrt(jnp.mean(oh * oh, axis=-1, keepdims=True) + EPS)
                gsl = slice(h * B_VAL_DIM, (h + 1) * B_VAL_DIM)
                o_ref[pl.ds(r0, C), gsl] = (oh * r * gn * gate_ref[pl.ds(r0, C), gsl]).astype(o_ref.dtype)
        return carry

    n_chunks = tt // C
    per_step = 2 if n_chunks % 2 == 0 else 1

    def step(j, carry):
        for u in range(per_step):
            chunk_body(j * per_step + u, carry)
        return carry

    lax.fori_loop(0, n_chunks // per_step, step, 0)

    @pl.when(i == pl.num_programs(1) - 1)
    def _():
        sout_ref[...] = s_ref[...]


def _gla(bq, bk, bv, lg, gate, gn, s0, tt, C):
    B, T, _ = bq.shape
    mcat, masks = _gla_consts(C)
    nl = int(np.log2(C))
    tok = lambda c: pl.BlockSpec((None, tt, c), lambda b, i: (b, i, 0))
    full = lambda shape: pl.BlockSpec(shape, lambda b, i: (0,) * len(shape))
    st = pl.BlockSpec((None, 2, LANES, 2 * B_VAL_DIM), lambda b, i: (b, 0, 0, 0))
    return pl.pallas_call(
        functools.partial(_gla_kernel, tt=tt, C=C),
        out_shape=[jax.ShapeDtypeStruct((B, T, 512), BF16),
                   jax.ShapeDtypeStruct((B, 2, LANES, 2 * B_VAL_DIM), F32)],
        grid=(B, T // tt),
        in_specs=[tok(256), tok(256), tok(512), tok(256), tok(512),
                  full(((nl + 2) * C, C)), full((nl + 1, 2 * C, C)), full((1, B_VAL_DIM)), st],
        out_specs=[tok(512), st],
        scratch_shapes=[pltpu.VMEM((2, LANES, 2 * B_VAL_DIM), F32)],
        compiler_params=_cparams(("arbitrary", "arbitrary")),
        name="gla",
    )(bq, bk, bv, lg, gate, jnp.asarray(mcat, BF16), jnp.asarray(masks, F32), gn, s0)


def _out_kernel(x_ref, oa_ref, ob_ref, oc_ref, wo_ref, g2_ref, h_ref, m_ref):
    acc = _dot(oa_ref[...], wo_ref[0:1024, :])
    acc = acc + _dot(ob_ref[...], wo_ref[1024:1536, :])
    acc = acc + _dot(oc_ref[...], wo_ref[1536:2048, :])
    h = x_ref[...] + acc
    h_ref[...] = h
    ms = jnp.mean(h * h, axis=-1, keepdims=True)
    m_ref[...] = (h * lax.rsqrt(ms + EPS) * g2_ref[...]).astype(BF16)


def _out_proj(x, oa, ob, oc, wo, g2, bm):
    M, D = x.shape
    row = lambda c: pl.BlockSpec((bm, c), lambda i: (i, 0))
    return pl.pallas_call(
        _out_kernel,
        out_shape=[jax.ShapeDtypeStruct((M, D), F32), jax.ShapeDtypeStruct((M, D), BF16)],
        grid=(M // bm,),
        in_specs=[row(D), row(1024), row(512), row(512),
                  pl.BlockSpec((D, D), lambda i: (0, 0)), pl.BlockSpec((1, D), lambda i: (0, 0))],
        out_specs=[row(D), row(D)],
        compiler_params=_cparams(("arbitrary",)),
        name="out_proj",
    )(x, oa, ob, oc, wo, g2.reshape(1, D))


def _ffn_kernel(h_ref, m_ref, wg_ref, wu_ref, wd_ref, y_ref):
    @pl.when(pl.program_id(1) == 0)
    def _():
        y_ref[...] = h_ref[...]

    m = m_ref[...]
    gate = _dot(m, wg_ref[...])
    up = _dot(m, wu_ref[...])
    a = (gate * (1.0 / (1.0 + jnp.exp(-gate))) * up).astype(BF16)
    y_ref[...] += _dot(a, wd_ref[...])


def _ffn(h, m, wg, wu, wd, bm, bf):
    M, D = h.shape
    Fh = wg.shape[1]
    return pl.pallas_call(
        _ffn_kernel,
        out_shape=jax.ShapeDtypeStruct((M, D), F32),
        grid=(M // bm, Fh // bf),
        in_specs=[pl.BlockSpec((bm, D), lambda i, j: (i, 0)),
                  pl.BlockSpec((bm, D), lambda i, j: (i, 0)),
                  pl.BlockSpec((D, bf), lambda i, j: (0, j)),
                  pl.BlockSpec((D, bf), lambda i, j: (0, j)),
                  pl.BlockSpec((bf, D), lambda i, j: (j, 0))],
        out_specs=pl.BlockSpec((bm, D), lambda i, j: (i, 0)),
        compiler_params=_cparams(("arbitrary", "arbitrary")),
        name="ffn",
    )(h, m, wg, wu, wd)


def _rope_tables(pos0, T):
    pos = (pos0 + jnp.arange(T, dtype=jnp.int32)).astype(F32)

    def cs(half):
        inv = ROPE_THETA ** (-jnp.arange(half, dtype=F32) / half)
        ang = pos[:, None] * inv[None, :]
        return jnp.cos(ang), jnp.sin(ang)

    c64, s64 = cs(A_HEAD_DIM // 2)
    c32, s32 = cs(IDX_DIM // 2)
    z32 = jnp.zeros_like(s32)
    return (jnp.concatenate([c64, c64], axis=1), jnp.concatenate([-s64, s64], axis=1),
            jnp.concatenate([c32, c32, c32, c32], axis=1),
            jnp.concatenate([-s32, z32, -s32, z32], axis=1),
            jnp.concatenate([z32, s32, z32, s32], axis=1))


def _pack_w_in(w_in):
    cols = []
    for name in _PACK_ORDER:
        so, w, _, pw = _LAYOUT[name]
        piece = w_in[:, :, so:so + w]
        if pw > w:
            piece = jnp.pad(piece, ((0, 0), (0, 0), (0, pw - w)))
        cols.append(piece)
    return jnp.concatenate(cols, axis=-1).astype(BF16)


def _state_to_groups(s):
    B = s.shape[0]
    s = s.reshape(B, 2, 2, B_KEY_DIM, B_VAL_DIM)
    z = jnp.zeros_like(s[:, :, 0])
    top = jnp.concatenate([s[:, :, 0], z], axis=-1)
    bot = jnp.concatenate([z, s[:, :, 1]], axis=-1)
    return jnp.concatenate([top, bot], axis=-2)


def _groups_to_state(sg):
    B = sg.shape[0]
    h0 = sg[:, :, :B_KEY_DIM, :B_VAL_DIM]
    h1 = sg[:, :, B_KEY_DIM:, B_VAL_DIM:]
    return jnp.stack([h0, h1], axis=2).reshape(B, B_HEADS, B_KEY_DIM, B_VAL_DIM)


def _layer(x, pos0, cache, prm):
    (g1, w_in_p, gq, gk, gki, w_alpha, b_alpha, g_gla, pool_w, pool_scale, w_o, g2, w_gate, w_up, w_down) = prm
    B, T, D = x.shape
    M = B * T
    x2 = x.reshape(M, D)
    bm = min(512, M)
    P = _norm_matmul(x2, g1, w_in_p, min(1024, M), PACKED_WIDTH // 3).reshape(B, T, PACKED_WIDTH)

    tt = min(512, T)
    tabs = _rope_tables(pos0, T)
    gki_p = jnp.pad(gki, (0, LANES - IDX_DIM)).reshape(1, LANES)
    w_alpha_p = jnp.pad(w_alpha, ((0, LANES - B_GATE_RANK), (0, 0))).astype(BF16)
    if cache is None:
        hist0 = jnp.zeros((B, HIST_ROWS, C_WIDTH), F32)
    else:
        hist0 = jnp.pad(cache[4], ((0, 0), (HIST_ROWS - POOL_HIST, 0), (0, 0)))
    (q, qi, k32, kb, vb, ki32, kieo, wt, bq, bk, bv, lg, gate, oc, ksq) = _post(
        P, tabs, gq.reshape(1, LANES), gk.reshape(1, LANES), gki_p, w_alpha_p, b_alpha.reshape(1, -1),
        pool_w.astype(BF16), pool_scale.reshape(1, -1), hist0, tt, pos0)

    v_off, _ = _seg("a_v")
    v32 = P[:, :, v_off:v_off + 256]
    u_off, _ = _seg("c_u")
    pool_state = P[:, T - POOL_HIST:, u_off:u_off + C_WIDTH]

    if cache is None:
        tq = 2 * LANES
        oa = _attn(q, qi, wt, kb, vb, kieo, ksq, tq, True, T, min(TOPK_MAX, T // 4))
        s0 = jnp.zeros((B, 2, LANES, 2 * B_VAL_DIM), F32)
        ob, sg = _gla(bq, bk, bv, lg, gate, g_gla.reshape(1, -1), s0, tt, CHUNK)
    else:
        ck, cv, cki, c_gla, _ = cache
        past = ck.shape[1]
        L = past + T
        S = -(-L // ATT_KB[1]) * ATT_KB[1]
        padk = lambda a: jnp.pad(a, ((0, 0), (0, S - L), (0, 0)))
        k_all = padk(jnp.concatenate([ck.reshape(B, past, 256).astype(BF16), kb], axis=1))
        v_all = padk(jnp.concatenate([cv.reshape(B, past, 256).astype(BF16), vb], axis=1))
        zc = jnp.zeros((B, past, IDX_DIM), BF16)
        cki_b = cki.astype(BF16)
        ki_all = padk(jnp.concatenate([jnp.concatenate([cki_b, zc, zc, cki_b], axis=-1), kieo], axis=1))
        tq = LANES
        padq = lambda a: jnp.pad(a, ((0, 0), (0, tq - T), (0, 0)))
        wt_p = jnp.pad(wt, ((0, 0), (0, 0), (0, tq - T)))
        oa = _attn(padq(q), padq(qi), wt_p, k_all, v_all, ki_all, None, tq, False, L,
                   min(TOPK_MAX, L // 4))[:, :T]
        ob, sg = _gla(bq, bk, bv, lg, gate, g_gla.reshape(1, -1), _state_to_groups(c_gla), tt, T)

    h, m = _out_proj(x2, oa.reshape(M, -1), ob.reshape(M, -1), oc.reshape(M, -1), w_o, g2, bm)
    y = _ffn(h, m, w_gate, w_up, w_down, min(1024, M), 512)
    state = (k32.reshape(B, T, A_KV_HEADS, A_HEAD_DIM), v32.reshape(B, T, A_KV_HEADS, A_HEAD_DIM),
             ki32, _groups_to_state(sg), pool_state)
    return y.reshape(B, T, D), state


def kernel(x_prompt, x_sample, cache_k, cache_v, cache_kidx, state_gla, state_pool,
           norm1, w_in, q_norm, k_norm, kidx_norm, w_alpha, b_alpha, gla_norm, pool_w, pool_scale,
           w_o, norm2, w_gate, w_up, w_down):
    depth = w_in.shape[0]
    past = cache_k.shape[2]
    w_in_p = _pack_w_in(w_in)
    w_o_b, w_gate_b, w_up_b, w_down_b = (w.astype(BF16) for w in (w_o, w_gate, w_up, w_down))
    yp, ys = x_prompt, x_sample
    st_p, st_s = [], []
    for l in range(depth):
        prm = (norm1[l], w_in_p[l], q_norm[l], k_norm[l], kidx_norm[l], w_alpha[l], b_alpha[l], gla_norm[l],
               pool_w[l], pool_scale[l], w_o_b[l], norm2[l], w_gate_b[l], w_up_b[l], w_down_b[l])
        yp, sp = _layer(yp, 0, None, prm)
        ys, ss = _layer(ys, past, (cache_k[l], cache_v[l], cache_kidx[l], state_gla[l], state_pool[l]), prm)
        st_p.append(sp)
        st_s.append(ss)
    stk = lambda sts, i: jnp.stack([s[i] for s in sts], axis=0)
    return (yp, ys,
            stk(st_p, 0), stk(st_p, 1), stk(st_p, 2), stk(st_p, 3), stk(st_p, 4),
            stk(st_s, 0), stk(st_s, 1), stk(st_s, 2), stk(st_s, 3), stk(st_s, 4))
```

```python
import functools

import numpy as np
import jax
import jax.numpy as jnp
from jax import lax
from jax.experimental import pallas as pl
from jax.experimental.pallas import tpu as pltpu

F32 = jnp.float32
BF16 = jnp.bfloat16

EPS = 1e-6
CHUNK = 64
TOPK_MAX = 256
ROPE_THETA = 10000.0
A_HEAD_DIM = 128
A_HEADS = 8
A_KV_HEADS = 2
IDX_HEADS = 16
IDX_DIM = 64
IDX_W_SCALE = (IDX_HEADS ** -0.5) * (IDX_DIM ** -0.5)
B_HEADS = 4
B_KEY_DIM = 64
B_VAL_DIM = 128
B_GATE_RANK = 16
B_GATE_TEMP = 16.0
C_WINDOWS = (2, 4, 8, 16)
C_WIDTH = 512
C_GROUP = 128
POOL_HIST = 15
HIST_ROWS = 16

LANES = 128
VMEM_LIMIT = 60 * 1024 * 1024

INT_MIN = -(2 ** 31)
IDX_SUB = 256
SEARCH_CLASSES = 4
ATT_KB = (2048, 512)
Q_SCALE = (A_HEAD_DIM ** -0.5) * float(np.log2(np.e))
BOUND_MARGIN = 1.02
BOUND_SAFE = 60.0

_SRC_SIZES = (("a_q", 1024), ("a_k", 256), ("a_v", 256), ("a_qi", 1024), ("a_ki", 64), ("a_wi", 16),
              ("b_q", 256), ("b_k", 256), ("b_v", 512), ("b_a", 16), ("b_g", 512), ("c_u", 512))
_PACK_ORDER = ("a_q", "a_qi", "b_v", "b_g", "c_u", "a_k", "a_v", "b_q", "b_k", "a_ki", "a_wi", "b_a")


def _build_layout():
    src, off = {}, 0
    for name, w in _SRC_SIZES:
        src[name] = (off, w)
        off += w
    lay, poff = {}, 0
    for name in _PACK_ORDER:
        so, w = src[name]
        pw = -(-w // LANES) * LANES
        lay[name] = (so, w, poff, pw)
        poff += pw
    return lay, poff


_LAYOUT, PACKED_WIDTH = _build_layout()


def _seg(name):
    _, _, poff, pw = _LAYOUT[name]
    return poff, pw


def _dot(a, b):
    return jnp.dot(a, b, preferred_element_type=F32)


def _dot_nt(a, b):
    return lax.dot_general(a, b, (((1,), (1,)), ((), ())), preferred_element_type=F32)


def _dot_tn(a, b):
    return lax.dot_general(a, b, (((0,), (0,)), ((), ())), preferred_element_type=F32)


def _bit_transpose32(rows):
    a = list(rows)
    j, m = 16, 0x0000FFFF
    while j:
        k = 0
        while k < 32:
            t = (a[k] ^ lax.shift_right_logical(a[k + j], jnp.int32(j))) & jnp.int32(m)
            a[k] = a[k] ^ t
            a[k + j] = a[k + j] ^ lax.shift_left(t, jnp.int32(j))
            k = (k + j + 1) & ~j
        j >>= 1
        m = (m ^ (m << j)) & 0xFFFFFFFF
    return a


def _cparams(sem):
    return pltpu.CompilerParams(dimension_semantics=sem, vmem_limit_bytes=VMEM_LIMIT)


def _norm_matmul_kernel(x_ref, g_ref, w_ref, o_ref, xn_ref):
    @pl.when(pl.program_id(1) == 0)
    def _():
        x = x_ref[...]
        ms = jnp.mean(x * x, axis=-1, keepdims=True)
        xn_ref[...] = (x * lax.rsqrt(ms + EPS) * g_ref[...]).astype(BF16)

    o_ref[...] = _dot(xn_ref[...], w_ref[...])


def _norm_matmul(x, g, w, bm, bn):
    M, D = x.shape
    N = w.shape[1]
    return pl.pallas_call(
        _norm_matmul_kernel,
        out_shape=jax.ShapeDtypeStruct((M, N), F32),
        grid=(M // bm, N // bn),
        in_specs=[pl.BlockSpec((bm, D), lambda i, j: (i, 0)),
                  pl.BlockSpec((1, D), lambda i, j: (0, 0)),
                  pl.BlockSpec((D, bn), lambda i, j: (0, j))],
        out_specs=pl.BlockSpec((bm, bn), lambda i, j: (i, j)),
        scratch_shapes=[pltpu.VMEM((bm, D), BF16)],
        compiler_params=_cparams(("arbitrary", "arbitrary")),
        name="norm_matmul",
    )(x, g.reshape(1, D), w)


def _post_kernel(p_ref, cosa_ref, sina_ref, cosb_ref, sinlo_ref, sinhi_ref, gq_ref, gk_ref, gki_ref,
                 walpha_ref, balpha_ref, poolw_ref, pscale_ref, hist0_ref,
                 q_ref, qi_ref, k32_ref, kb_ref, vb_ref, ki32_ref, kieo_ref, wt_ref,
                 bq_ref, bk_ref, bv_ref, lg_ref, gate_ref, oc_ref, ksq_ref,
                 uext_ref, hist_ref, *, tt, pos0):
    i = pl.program_id(1)

    def seg(name):
        off, w = _seg(name)
        return p_ref[:, off:off + w]

    cosa, sina = cosa_ref[...], sina_ref[...]

    def head_norm_rope(a, g):
        r = lax.rsqrt(jnp.mean(a * a, axis=-1, keepdims=True) + EPS)
        y = a * r * g
        return y * cosa + pltpu.roll(y, 64, 1) * sina

    aq = seg("a_q")
    gq = gq_ref[...]
    for h in range(A_HEADS):
        sl = slice(h * LANES, (h + 1) * LANES)
        q_ref[:, sl] = (head_norm_rope(aq[:, sl], gq) * Q_SCALE).astype(BF16)
    ak = seg("a_k")
    gk = gk_ref[...]
    ksq = jnp.zeros((tt, 1), F32)
    for h in range(A_KV_HEADS):
        sl = slice(h * LANES, (h + 1) * LANES)
        kk = head_norm_rope(ak[:, sl], gk)
        k32_ref[:, sl] = kk
        kb_ref[:, sl] = kk.astype(BF16)
        ksq = jnp.maximum(ksq, jnp.sum(kk * kk, axis=-1, keepdims=True))
    ksq_ref[...] = jnp.broadcast_to(jnp.max(ksq, axis=0, keepdims=True), ksq_ref.shape)
    vb_ref[...] = seg("a_v").astype(BF16)

    cosb, sinlo, sinhi = cosb_ref[...], sinlo_ref[...], sinhi_ref[...]

    def rope64(y):
        return y * cosb + pltpu.roll(y, 96, 1) * sinlo + pltpu.roll(y, 32, 1) * sinhi

    aqi = seg("a_qi")
    for gidx in range(IDX_HEADS // 2):
        sl = slice(gidx * LANES, (gidx + 1) * LANES)
        qi_ref[:, sl] = rope64(aqi[:, sl]).astype(BF16)
    aki = seg("a_ki")
    rki = lax.rsqrt(jnp.sum(aki * aki, axis=-1, keepdims=True) * (1.0 / IDX_DIM) + EPS)
    ki = rope64(aki * rki * gki_ref[...])
    ki32_ref[...] = ki[:, :IDX_DIM]
    kieo_ref[:, 0:LANES] = ki.astype(BF16)
    kieo_ref[:, LANES:2 * LANES] = pltpu.roll(ki, 64, 1).astype(BF16)

    awi = seg("a_wi") * IDX_W_SCALE
    if tt % LANES == 0:
        for c in range(tt // LANES):
            wt_ref[:, c * LANES:(c + 1) * LANES] = awi[c * LANES:(c + 1) * LANES, :].T[:IDX_HEADS, :]
    else:
        padded = jnp.concatenate([awi, jnp.zeros((LANES - tt, LANES), F32)], axis=0)
        wt_ref[...] = padded.T[:IDX_HEADS, :tt]

    bq_ref[...] = seg("b_q") * (B_KEY_DIM ** -0.5)
    bk_ref[...] = seg("b_k")
    bv_ref[...] = seg("b_v").astype(BF16)
    z = _dot(seg("b_a").astype(BF16), walpha_ref[...]) + balpha_ref[...]
    lg_ref[...] = (jnp.minimum(z, 0.0) - jnp.log1p(jnp.exp(-jnp.abs(z)))) * (1.0 / B_GATE_TEMP)
    bg = seg("b_g")
    gate_ref[...] = bg * (1.0 / (1.0 + jnp.exp(-bg)))

    @pl.when(i == 0)
    def _():
        hist_ref[...] = hist0_ref[...]

    u = seg("c_u")
    uext_ref[0:HIST_ROWS, :] = hist_ref[...]
    uext_ref[HIST_ROWS:HIST_ROWS + tt, :] = u
    hist_ref[...] = uext_ref[tt:tt + HIST_ROWS, :]
    pos = pos0 + i * tt + lax.broadcasted_iota(jnp.int32, (tt, 1), 0)
    for gi, w in enumerate(C_WINDOWS):
        sl = slice(gi * C_GROUP, (gi + 1) * C_GROUP)
        s = uext_ref[:, sl]
        shift = 1
        while shift < w:
            s = s + pltpu.roll(s, shift, 0)
            shift *= 2
        cnt = jnp.minimum(w, pos + 1).astype(F32)
        d = s[HIST_ROWS:, :] / cnt - u[:, sl]
        y = _dot(d.astype(BF16), poolw_ref[gi]) * pscale_ref[:, sl]
        oc_ref[:, sl] = y.astype(BF16)


def _post(P, tabs, gq, gk, gki, w_alpha, b_alpha, pool_w, pool_scale, hist0, tt, pos0):
    B, T, W = P.shape
    nT = T // tt
    tok = lambda c: pl.BlockSpec((None, tt, c), lambda b, i: (b, i, 0))
    tab = pl.BlockSpec((tt, LANES), lambda b, i: (i, 0))
    full = lambda shape: pl.BlockSpec(shape, lambda b, i: (0,) * len(shape))
    out_shape = [
        jax.ShapeDtypeStruct((B, T, 1024), BF16),
        jax.ShapeDtypeStruct((B, T, 1024), BF16),
        jax.ShapeDtypeStruct((B, T, 256), F32),
        jax.ShapeDtypeStruct((B, T, 256), BF16),
        jax.ShapeDtypeStruct((B, T, 256), BF16),
        jax.ShapeDtypeStruct((B, T, IDX_DIM), F32),
        jax.ShapeDtypeStruct((B, T, 256), BF16),
        jax.ShapeDtypeStruct((B, IDX_HEADS, T), F32),
        jax.ShapeDtypeStruct((B, T, 256), F32),
        jax.ShapeDtypeStruct((B, T, 256), F32),
        jax.ShapeDtypeStruct((B, T, 512), BF16),
        jax.ShapeDtypeStruct((B, T, 256), F32),
        jax.ShapeDtypeStruct((B, T, 512), F32),
        jax.ShapeDtypeStruct((B, T, 512), BF16),
        jax.ShapeDtypeStruct((B, nT, 8, LANES), F32),
    ]
    out_specs = [tok(1024), tok(1024), tok(256), tok(256), tok(256), tok(IDX_DIM), tok(256),
                 pl.BlockSpec((None, IDX_HEADS, tt), lambda b, i: (b, 0, i)),
                 tok(256), tok(256), tok(512), tok(256), tok(512), tok(512),
                 pl.BlockSpec((None, None, 8, LANES), lambda b, i: (b, i, 0, 0))]
    return pl.pallas_call(
        functools.partial(_post_kernel, tt=tt, pos0=pos0),
        out_shape=out_shape,
        grid=(B, nT),
        in_specs=[tok(W), tab, tab, tab, tab, tab,
                  full((1, LANES)), full((1, LANES)), full((1, LANES)),
                  full((LANES, 256)), full((1, 256)),
                  full((len(C_WINDOWS), C_GROUP, C_GROUP)), full((1, C_WIDTH)),
                  pl.BlockSpec((None, HIST_ROWS, C_WIDTH), lambda b, i: (b, 0, 0))],
        out_specs=out_specs,
        scratch_shapes=[pltpu.VMEM((tt + HIST_ROWS, C_WIDTH), F32), pltpu.VMEM((HIST_ROWS, C_WIDTH), F32)],
        compiler_params=_cparams(("arbitrary", "arbitrary")),
        name="post",
    )(P, *tabs, gq, gk, gki, w_alpha, b_alpha, pool_w, pool_scale, hist0)


def _attn_kernel(*refs, tq, kb_sizes, causal, n_keys, topk, has_bound):
    q_ref, qi_ref, wt_ref, k_ref, v_ref, ki_ref = refs[:6]
    ksq_ref = refs[6] if has_bound else None
    o_ref, key_ref, plane_ref, acc_ref, m_ref, l_ref = refs[-6:]
    i = pl.program_id(1)

    @pl.when(i == 0)
    def _():
        plane_ref[...] = jnp.zeros(plane_ref.shape, jnp.int32)

    n_slabs = IDX_HEADS // 2
    kb_big, kb_small = kb_sizes
    if causal:
        n_vis = (i + 1) * tq
        t_abs = i * tq + lax.broadcasted_iota(jnp.int32, (1, tq), 1)
        limit = (t_abs // CHUNK + 1) * CHUNK
    else:
        n_vis = n_keys
        limit = jnp.full((1, tq), n_keys, jnp.int32)
    n_big = (n_vis - 1) // kb_big
    n_small = (n_vis - n_big * kb_big + kb_small - 1) // kb_small
    n_rows = n_big * kb_big + n_small * kb_small

    def for_blocks(body, carry):
        def run(size, first, count, c):
            return lax.fori_loop(0, count, lambda j, cc: body(pl.multiple_of(first + j * size, size), size, cc), c)

        return run(kb_small, n_big * kb_big, n_small, run(kb_big, 0, n_big, carry))

    q2 = jnp.concatenate([qi_ref[:, g * LANES:(g + 1) * LANES] for g in range(n_slabs)], axis=0)

    def idx_body(off0, size, carry):
        for sb in range(size // IDX_SUB):
            off = pl.multiple_of(off0 + sb * IDX_SUB, IDX_SUB)
            ye = _dot_nt(ki_ref[pl.ds(off, IDX_SUB), 0:LANES], q2)
            yo = _dot_nt(ki_ref[pl.ds(off, IDX_SUB), LANES:2 * LANES], q2)
            score = jnp.zeros((IDX_SUB, tq), F32)
            for g in range(n_slabs):
                score = score + jnp.maximum(ye[:, g * tq:(g + 1) * tq], 0.0) * wt_ref[2 * g:2 * g + 1, :]
                score = score + jnp.maximum(yo[:, g * tq:(g + 1) * tq], 0.0) * wt_ref[2 * g + 1:2 * g + 2, :]
            bits = pltpu.bitcast(score, jnp.int32)
            key = bits ^ ((bits >> 31) & jnp.int32(0x7FFFFFFF))
            if size == kb_small:
                s_idx = off + lax.broadcasted_iota(jnp.int32, (IDX_SUB, 1), 0)
                key = jnp.where(s_idx < limit, key, jnp.int32(INT_MIN))
            key_ref[pl.ds(off, IDX_SUB), :] = key
            ukey = key ^ jnp.int32(INT_MIN)
            planes = _bit_transpose32([ukey[8 * v:8 * v + 8, :] for v in range(32)])
            grp = pl.multiple_of(off // 32, 8)
            for p in range(32):
                plane_ref[p, pl.ds(grp, 8), :] = planes[p]
        return carry

    for_blocks(idx_body, 0)

    n_words = key_ref.shape[0] // 32

    def popcount_rows(x):
        part = jnp.sum(lax.population_count(x).reshape(x.shape[0] // 8, 8, tq), axis=0)
        return jnp.sum(part, axis=0, keepdims=True)

    def search(nw):
        def run():
            word_row = lax.broadcasted_iota(jnp.int32, (nw, tq), 0)
            in_play = jnp.where(word_row < n_rows // 32, jnp.int32(-1), jnp.int32(0))

            def bit_body(p, carry):
                cand, above, ukth = carry
                plane = plane_ref[p, pl.ds(0, nw), :]
                ones = cand & plane
                c1 = popcount_rows(ones)
                take = above + c1 >= topk
                cand = jnp.where(take, ones, cand & ~plane)
                above = jnp.where(take, above, above + c1)
                ukth = jnp.where(take, ukth | lax.shift_left(jnp.int32(1), 31 - p), ukth)
                return cand, above, ukth

            zero_row = jnp.zeros((1, tq), jnp.int32)
            at_kth, above, ukth = lax.fori_loop(0, 32, bit_body, (in_play, zero_row, zero_row))
            return above, ukth, popcount_rows(at_kth)

        return run

    if causal and n_words % (8 * SEARCH_CLASSES) == 0:
        step = n_words // SEARCH_CLASSES
        cls = (n_rows // 32 + step - 1) // step - 1
        above, ukth, n_at = lax.switch(cls, [search(step * (c + 1)) for c in range(SEARCH_CLASSES)])
    else:
        above, ukth, n_at = search(n_words)()
    tau = jnp.maximum(ukth ^ jnp.int32(INT_MIN), jnp.int32(INT_MIN + 1))

    def count32(pred):
        def body(off, size, acc):
            s_idx = off + lax.broadcasted_iota(jnp.int32, (size, 1), 0)
            hit = jnp.where(pred(key_ref[pl.ds(off, size), :], s_idx), 1, 0)
            return acc + jnp.sum(hit.reshape(size // 8, 8, tq), axis=0)

        return jnp.sum(for_blocks(body, jnp.zeros((8, tq), jnp.int32)), axis=0, keepdims=True)

    n_ge = jnp.where(ukth != 0, above + n_at, 0)

    @pl.when(jnp.max(n_ge) > topk)
    def _():
        quota = topk - count32(lambda k, s: k > tau)
        idx_bits = int(key_ref.shape[0]).bit_length()

        def bit_body(bi, cut):
            cand = cut | lax.shift_left(jnp.int32(1), idx_bits - 1 - bi)
            kept = count32(lambda k, s: jnp.where(k == tau, s, cand) < cand)
            return jnp.where(kept <= quota, cand, cut)

        cut = lax.fori_loop(0, idx_bits, bit_body, jnp.zeros((1, tq), jnp.int32))

        def demote(off, size, carry):
            s_idx = off + lax.broadcasted_iota(jnp.int32, (size, 1), 0)
            blk = key_ref[pl.ds(off, size), :]
            key_ref[pl.ds(off, size), :] = jnp.where(jnp.where(blk == tau, s_idx, -1) >= cut, tau - 1, blk)
            return carry

        for_blocks(demote, 0)

    hpg = A_HEADS // A_KV_HEADS
    qgs = [jnp.concatenate([q_ref[:, (g * hpg + h) * LANES:(g * hpg + h + 1) * LANES] for h in range(hpg)], axis=0)
           for g in range(A_KV_HEADS)]
    l_ref[...] = jnp.zeros(l_ref.shape, F32)
    acc_ref[...] = jnp.zeros(acc_ref.shape, F32)

    def masked_scores(off, size, g, sel, fill, shift):
        s = _dot_nt(k_ref[pl.ds(off, size), g * LANES:(g + 1) * LANES], qgs[g])
        return jnp.concatenate([jnp.where(sel, shift(s[:, h * tq:(h + 1) * tq], h), fill) for h in range(hpg)], axis=1)

    def add_pv(off, size, g, p):
        return _dot_tn(v_ref[pl.ds(off, size), g * LANES:(g + 1) * LANES], p.astype(BF16))

    def online():
        m_ref[...] = jnp.full(m_ref.shape, -1e30, F32)

        def body(off, size, carry):
            sel = key_ref[pl.ds(off, size), :] >= tau
            for g in range(A_KV_HEADS):
                s = masked_scores(off, size, g, sel, -jnp.inf, lambda x, h: x)
                m_old = m_ref[g]
                m_new = jnp.maximum(m_old, jnp.max(s, axis=0, keepdims=True))
                alpha = jnp.exp2(m_old - m_new)
                p = jnp.exp2(s - m_new)
                l_ref[g] = alpha * l_ref[g] + jnp.sum(p, axis=0, keepdims=True)
                acc_ref[g] = acc_ref[g] * alpha + add_pv(off, size, g, p)
                m_ref[g] = m_new
            return carry

        for_blocks(body, 0)

    def bounded(bounds):
        def body(off, size, carry):
            sel = key_ref[pl.ds(off, size), :] >= tau
            for g in range(A_KV_HEADS):
                b = bounds[g]
                p = masked_scores(off, size, g, sel, 0.0, lambda x, h: jnp.exp2(x - b[:, h * tq:(h + 1) * tq]))
                l_ref[g] = l_ref[g] + jnp.sum(p, axis=0, keepdims=True)
                acc_ref[g] = acc_ref[g] + add_pv(off, size, g, p)
            return carry

        for_blocks(body, 0)

    if ksq_ref is None:
        online()
    else:
        kmax = jnp.sqrt(jnp.max(ksq_ref[...]))
        ones = jnp.ones((8, LANES), BF16)
        bounds = [jnp.sqrt(_dot_nt(ones, qg * qg)[0:1, :]) * (kmax * BOUND_MARGIN) for qg in qgs]
        safe = jnp.max(jnp.maximum(bounds[0], bounds[1])) <= BOUND_SAFE

        @pl.when(safe)
        def _():
            bounded(bounds)

        @pl.when(jnp.logical_not(safe))
        def _():
            online()

    for g in range(A_KV_HEADS):
        o = acc_ref[g] / l_ref[g]
        for h in range(hpg):
            hh = g * hpg + h
            o_ref[:, hh * LANES:(hh + 1) * LANES] = o[:, h * tq:(h + 1) * tq].T.astype(o_ref.dtype)


def _attn(q, qi, wt, kb, vb, kieo, ksq, tq, causal, n_keys, topk):
    B, T, _ = q.shape
    S = kb.shape[1]
    assert S % ATT_KB[1] == 0 and ATT_KB[0] % ATT_KB[1] == 0 and ATT_KB[1] % IDX_SUB == 0
    hpg = A_HEADS // A_KV_HEADS
    kern = functools.partial(_attn_kernel, tq=tq, kb_sizes=ATT_KB, causal=causal, n_keys=n_keys, topk=topk,
                             has_bound=ksq is not None)
    per_batch = pl.BlockSpec((None, S, 256), lambda b, i: (b, 0, 0))
    in_specs = [pl.BlockSpec((None, tq, 1024), lambda b, i: (b, i, 0)),
                pl.BlockSpec((None, tq, 1024), lambda b, i: (b, i, 0)),
                pl.BlockSpec((None, IDX_HEADS, tq), lambda b, i: (b, 0, i)),
                per_batch, per_batch, per_batch]
    args = [q, qi, wt, kb, vb, kieo]
    if ksq is not None:
        in_specs.append(pl.BlockSpec((None,) + ksq.shape[1:], lambda b, i: (b, 0, 0, 0)))
        args.append(ksq)
    return pl.pallas_call(
        kern,
        out_shape=jax.ShapeDtypeStruct((B, T, 1024), BF16),
        grid=(B, T // tq),
        in_specs=in_specs,
        out_specs=pl.BlockSpec((None, tq, 1024), lambda b, i: (b, i, 0)),
        scratch_shapes=[pltpu.VMEM((S, tq), jnp.int32),
                        pltpu.VMEM((32, S // 32, tq), jnp.int32),
                        pltpu.VMEM((A_KV_HEADS, A_HEAD_DIM, hpg * tq), F32),
                        pltpu.VMEM((A_KV_HEADS, 1, hpg * tq), F32),
                        pltpu.VMEM((A_KV_HEADS, 1, hpg * tq), F32)],
        compiler_params=_cparams(("arbitrary", "arbitrary")),
        name="attn",
    )(*args)


def _gla_consts(C):
    nl = int(np.log2(C))
    M = np.zeros((nl + 2, C, C), np.float32)
    masks = np.zeros((nl + 1, C, C), np.float32)
    for j in range(nl):
        m = 1 << j
        for t in range(C):
            blk = (t // (2 * m)) * 2 * m
            mid = blk + m
            if t >= mid:
                M[j, t, mid:t + 1] = 1
                masks[j, t, blk:mid] = 1
            else:
                M[j, t, t + 1:mid] = 1
    for t in range(C):
        M[nl, t, :t + 1] = 1
        M[nl + 1, t, t + 1:] = 1
    masks[nl] = np.eye(C)
    return M.reshape((nl + 2) * C, C), np.concatenate([masks, masks], axis=1)


def _gla_kernel(q_ref, k_ref, v_ref, lg_ref, gate_ref, mcat_ref, masks_ref, gn_ref, s0_ref,
                o_ref, sout_ref, s_ref, *, tt, C):
    i = pl.program_id(1)
    nl = int(np.log2(C))

    @pl.when(i == 0)
    def _():
        s_ref[...] = s0_ref[...]

    lane = lax.broadcasted_iota(jnp.int32, (1, LANES), 1)
    low = lane < B_KEY_DIM
    row = lax.broadcasted_iota(jnp.int32, (C, 1), 0)
    rr = lax.broadcasted_iota(jnp.int32, (LANES, 2 * B_VAL_DIM), 0)
    cc = lax.broadcasted_iota(jnp.int32, (LANES, 2 * B_VAL_DIM), 1)
    blockdiag = (rr < B_KEY_DIM) == (cc < B_VAL_DIM)
    gn = gn_ref[...]

    def stack2(x):
        return jnp.concatenate([jnp.where(low, x, 0.0), jnp.where(low, 0.0, x)], axis=0).astype(BF16)

    def chunk_body(c, carry):
        r0 = pl.multiple_of(c * C, C)
        qc = q_ref[pl.ds(r0, C), :]
        kc = k_ref[pl.ds(r0, C), :]
        vc = v_ref[pl.ds(r0, C), :]
        lgc = lg_ref[pl.ds(r0, C), :]
        hi = lgc.astype(BF16)
        r1 = lgc - hi.astype(F32)
        mid = r1.astype(BF16)
        lo = (r1 - mid.astype(F32)).astype(BF16)
        mcat = mcat_ref[...]
        E = jnp.exp(_dot(mcat, hi) + _dot(mcat, mid) + _dot(mcat, lo))
        e_b = E[nl * C:(nl + 1) * C, :]
        e_k = E[(nl + 1) * C:(nl + 2) * C, :]
        qt = qc * e_b
        kt = kc * e_k
        e_last = e_b[C - 1:C, :]
        for g in range(B_HEADS // 2):
            sl = slice(g * LANES, (g + 1) * LANES)
            att = _dot_nt(stack2(qc[:, sl]), kc[:, sl].astype(BF16)) * masks_ref[nl]
            for j in range(nl):
                right = ((row >> j) & 1) == 1
                x = jnp.where(right, qc[:, sl], kc[:, sl]) * E[j * C:(j + 1) * C, sl]
                att = att + _dot_nt(stack2(x), x.astype(BF16)) * masks_ref[j]
            att = att.astype(BF16)
            vg = vc[:, g * 2 * B_VAL_DIM:(g + 1) * 2 * B_VAL_DIM]
            sg = s_ref[g]
            o_intra = jnp.concatenate([_dot(att[0:C], vg[:, 0:B_VAL_DIM]),
                                       _dot(att[C:2 * C], vg[:, B_VAL_DIM:2 * B_VAL_DIM])], axis=1)
            o = o_intra + _dot(qt[:, sl].astype(BF16), sg.astype(BF16))
            upd = _dot_tn(kt[:, sl].astype(BF16), vg)
            dec = jnp.broadcast_to(e_last[:, sl], (LANES, LANES)).T
            dec = jnp.concatenate([dec, dec], axis=1)
            s_ref[g] = dec * sg + jnp.where(blockdiag, upd, 0.0)
            for hh in range(2):
                h = 2 * g + hh
                oh = o[:, hh * B_VAL_DIM:(hh + 1) * B_VAL_DIM]
                r = lax.rsqrt(jnp.mean(oh * oh, axis=-1, keepdims=True) + EPS)
                gsl = slice(h * B_VAL_DIM, (h + 1) * B_VAL_DIM)
                o_ref[pl.ds(r0, C), gsl] = (oh * r * gn * gate_ref[pl.ds(r0, C), gsl]).astype(o_ref.dtype)
        return carry

    n_chunks = tt // C
    per_step = 2 if n_chunks % 2 == 0 else 1

    def step(j, carry):
        for u in range(per_step):
            chunk_body(j * per_step + u, carry)
        return carry

    lax.fori_loop(0, n_chunks // per_step, step, 0)

    @pl.when(i == pl.num_programs(1) - 1)
    def _():
        sout_ref[...] = s_ref[...]


def _gla(bq, bk, bv, lg, gate, gn, s0, tt, C):
    B, T, _ = bq.shape
    mcat, masks = _gla_consts(C)
    nl = int(np.log2(C))
    tok = lambda c: pl.BlockSpec((None, tt, c), lambda b, i: (b, i, 0))
    full = lambda shape: pl.BlockSpec(shape, lambda b, i: (0,) * len(shape))
    st = pl.BlockSpec((None, 2, LANES, 2 * B_VAL_DIM), lambda b, i: (b, 0, 0, 0))
    return pl.pallas_call(
        functools.partial(_gla_kernel, tt=tt, C=C),
        out_shape=[jax.ShapeDtypeStruct((B, T, 512), BF16),
                   jax.ShapeDtypeStruct((B, 2, LANES, 2 * B_VAL_DIM), F32)],
        grid=(B, T // tt),
        in_specs=[tok(256), tok(256), tok(512), tok(256), tok(512),
                  full(((nl + 2) * C, C)), full((nl + 1, 2 * C, C)), full((1, B_VAL_DIM)), st],
        out_specs=[tok(512), st],
        scratch_shapes=[pltpu.VMEM((2, LANES, 2 * B_VAL_DIM), F32)],
        compiler_params=_cparams(("arbitrary", "arbitrary")),
        name="gla",
    )(bq, bk, bv, lg, gate, jnp.asarray(mcat, BF16), jnp.asarray(masks, F32), gn, s0)


def _out_kernel(x_ref, oa_ref, ob_ref, oc_ref, wo_ref, g2_ref, h_ref, m_ref):
    acc = _dot(oa_ref[...], wo_ref[0:1024, :])
    acc = acc + _dot(ob_ref[...], wo_ref[1024:1536, :])
    acc = acc + _dot(oc_ref[...], wo_ref[1536:2048, :])
    h = x_ref[...] + acc
    h_ref[...] = h
    ms = jnp.mean(h * h, axis=-1, keepdims=True)
    m_ref[...] = (h * lax.rsqrt(ms + EPS) * g2_ref[...]).astype(BF16)


def _out_proj(x, oa, ob, oc, wo, g2, bm):
    M, D = x.shape
    row = lambda c: pl.BlockSpec((bm, c), lambda i: (i, 0))
    return pl.pallas_call(
        _out_kernel,
        out_shape=[jax.ShapeDtypeStruct((M, D), F32), jax.ShapeDtypeStruct((M, D), BF16)],
        grid=(M // bm,),
        in_specs=[row(D), row(1024), row(512), row(512),
                  pl.BlockSpec((D, D), lambda i: (0, 0)), pl.BlockSpec((1, D), lambda i: (0, 0))],
        out_specs=[row(D), row(D)],
        compiler_params=_cparams(("arbitrary",)),
        name="out_proj",
    )(x, oa, ob, oc, wo, g2.reshape(1, D))


def _ffn_kernel(h_ref, m_ref, wg_ref, wu_ref, wd_ref, y_ref):
    @pl.when(pl.program_id(1) == 0)
    def _():
        y_ref[...] = h_ref[...]

    m = m_ref[...]
    gate = _dot(m, wg_ref[...])
    up = _dot(m, wu_ref[...])
    a = (gate * (1.0 / (1.0 + jnp.exp(-gate))) * up).astype(BF16)
    y_ref[...] += _dot(a, wd_ref[...])


def _ffn(h, m, wg, wu, wd, bm, bf):
    M, D = h.shape
    Fh = wg.shape[1]
    return pl.pallas_call(
        _ffn_kernel,
        out_shape=jax.ShapeDtypeStruct((M, D), F32),
        grid=(M // bm, Fh // bf),
        in_specs=[pl.BlockSpec((bm, D), lambda i, j: (i, 0)),
                  pl.BlockSpec((bm, D), lambda i, j: (i, 0)),
                  pl.BlockSpec((D, bf), lambda i, j: (0, j)),
                  pl.BlockSpec((D, bf), lambda i, j: (0, j)),
                  pl.BlockSpec((bf, D), lambda i, j: (j, 0))],
        out_specs=pl.BlockSpec((bm, D), lambda i, j: (i, 0)),
        compiler_params=_cparams(("arbitrary", "arbitrary")),
        name="ffn",
    )(h, m, wg, wu, wd)


def _rope_tables(pos0, T):
    pos = (pos0 + jnp.arange(T, dtype=jnp.int32)).astype(F32)

    def cs(half):
        inv = ROPE_THETA ** (-jnp.arange(half, dtype=F32) / half)
        ang = pos[:, None] * inv[None, :]
        return jnp.cos(ang), jnp.sin(ang)

    c64, s64 = cs(A_HEAD_DIM // 2)
    c32, s32 = cs(IDX_DIM // 2)
    z32 = jnp.zeros_like(s32)
    return (jnp.concatenate([c64, c64], axis=1), jnp.concatenate([-s64, s64], axis=1),
            jnp.concatenate([c32, c32, c32, c32], axis=1),
            jnp.concatenate([-s32, z32, -s32, z32], axis=1),
            jnp.concatenate([z32, s32, z32, s32], axis=1))


def _pack_w_in(w_in):
    cols = []
    for name in _PACK_ORDER:
        so, w, _, pw = _LAYOUT[name]
        piece = w_in[:, :, so:so + w]
        if pw > w:
            piece = jnp.pad(piece, ((0, 0), (0, 0), (0, pw - w)))
        cols.append(piece)
    return jnp.concatenate(cols, axis=-1).astype(BF16)


def _state_to_groups(s):
    B = s.shape[0]
    s = s.reshape(B, 2, 2, B_KEY_DIM, B_VAL_DIM)
    z = jnp.zeros_like(s[:, :, 0])
    top = jnp.concatenate([s[:, :, 0], z], axis=-1)
    bot = jnp.concatenate([z, s[:, :, 1]], axis=-1)
    return jnp.concatenate([top, bot], axis=-2)


def _groups_to_state(sg):
    B = sg.shape[0]
    h0 = sg[:, :, :B_KEY_DIM, :B_VAL_DIM]
    h1 = sg[:, :, B_KEY_DIM:, B_VAL_DIM:]
    return jnp.stack([h0, h1], axis=2).reshape(B, B_HEADS, B_KEY_DIM, B_VAL_DIM)


def _layer(x, pos0, cache, prm):
    (g1, w_in_p, gq, gk, gki, w_alpha, b_alpha, g_gla, pool_w, pool_scale, w_o, g2, w_gate, w_up, w_down) = prm
    B, T, D = x.shape
    M = B * T
    x2 = x.reshape(M, D)
    bm = min(512, M)
    P = _norm_matmul(x2, g1, w_in_p, min(1024, M), PACKED_WIDTH // 3).reshape(B, T, PACKED_WIDTH)

    tt = min(512, T)
    tabs = _rope_tables(pos0, T)
    gki_p = jnp.pad(gki, (0, LANES - IDX_DIM)).reshape(1, LANES)
    w_alpha_p = jnp.pad(w_alpha, ((0, LANES - B_GATE_RANK), (0, 0))).astype(BF16)
    if cache is None:
        hist0 = jnp.zeros((B, HIST_ROWS, C_WIDTH), F32)
    else:
        hist0 = jnp.pad(cache[4], ((0, 0), (HIST_ROWS - POOL_HIST, 0), (0, 0)))
    (q, qi, k32, kb, vb, ki32, kieo, wt, bq, bk, bv, lg, gate, oc, ksq) = _post(
        P, tabs, gq.reshape(1, LANES), gk.reshape(1, LANES), gki_p, w_alpha_p, b_alpha.reshape(1, -1),
        pool_w.astype(BF16), pool_scale.reshape(1, -1), hist0, tt, pos0)

    v_off, _ = _seg("a_v")
    v32 = P[:, :, v_off:v_off + 256]
    u_off, _ = _seg("c_u")
    pool_state = P[:, T - POOL_HIST:, u_off:u_off + C_WIDTH]

    if cache is None:
        tq = 2 * LANES
        oa = _attn(q, qi, wt, kb, vb, kieo, ksq, tq, True, T, min(TOPK_MAX, T // 4))
        s0 = jnp.zeros((B, 2, LANES, 2 * B_VAL_DIM), F32)
        ob, sg = _gla(bq, bk, bv, lg, gate, g_gla.reshape(1, -1), s0, tt, CHUNK)
    else:
        ck, cv, cki, c_gla, _ = cache
        past = ck.shape[1]
        L = past + T
        S = -(-L // ATT_KB[1]) * ATT_KB[1]
        padk = lambda a: jnp.pad(a, ((0, 0), (0, S - L), (0, 0)))
        k_all = padk(jnp.concatenate([ck.reshape(B, past, 256).astype(BF16), kb], axis=1))
        v_all = padk(jnp.concatenate([cv.reshape(B, past, 256).astype(BF16), vb], axis=1))
        zc = jnp.zeros((B, past, IDX_DIM), BF16)
        cki_b = cki.astype(BF16)
        ki_all = padk(jnp.concatenate([jnp.concatenate([cki_b, zc, zc, cki_b], axis=-1), kieo], axis=1))
        tq = LANES
        padq = lambda a: jnp.pad(a, ((0, 0), (0, tq - T), (0, 0)))
        wt_p = jnp.pad(wt, ((0, 0), (0, 0), (0, tq - T)))
        oa = _attn(padq(q), padq(qi), wt_p, k_all, v_all, ki_all, None, tq, False, L,
                   min(TOPK_MAX, L // 4))[:, :T]
        ob, sg = _gla(bq, bk, bv, lg, gate, g_gla.reshape(1, -1), _state_to_groups(c_gla), tt, T)

    h, m = _out_proj(x2, oa.reshape(M, -1), ob.reshape(M, -1), oc.reshape(M, -1), w_o, g2, bm)
    y = _ffn(h, m, w_gate, w_up, w_down, min(1024, M), 512)
    state = (k32.reshape(B, T, A_KV_HEADS, A_HEAD_DIM), v32.reshape(B, T, A_KV_HEADS, A_HEAD_DIM),
             ki32, _groups_to_state(sg), pool_state)
    return y.reshape(B, T, D), state


def kernel(x_prompt, x_sample, cache_k, cache_v, cache_kidx, state_gla, state_pool,
           norm1, w_in, q_norm, k_norm, kidx_norm, w_alpha, b_alpha, gla_norm, pool_w, pool_scale,
           w_o, norm2, w_gate, w_up, w_down):
    depth = w_in.shape[0]
    past = cache_k.shape[2]
    w_in_p = _pack_w_in(w_in)
    w_o_b, w_gate_b, w_up_b, w_down_b = (w.astype(BF16) for w in (w_o, w_gate, w_up, w_down))
    yp, ys = x_prompt, x_sample
    st_p, st_s = [], []
    for l in range(depth):
        prm = (norm1[l], w_in_p[l], q_norm[l], k_norm[l], kidx_norm[l], w_alpha[l], b_alpha[l], gla_norm[l],
               pool_w[l], pool_scale[l], w_o_b[l], norm2[l], w_gate_b[l], w_up_b[l], w_down_b[l])
        yp, sp = _layer(yp, 0, None, prm)
        ys, ss = _layer(ys, past, (cache_k[l], cache_v[l], cache_kidx[l], state_gla[l], state_pool[l]), prm)
        st_p.append(sp)
        st_s.append(ss)
    stk = lambda sts, i: jnp.stack([s[i] for s in sts], axis=0)
    return (yp, ys,
            stk(st_p, 0), stk(st_p, 1), stk(st_p, 2), stk(st_p, 3), stk(st_p, 4),
            stk(st_s, 0), stk(st_s, 1), stk(st_s, 2), stk(st_s, 3), stk(st_s, 4))
```

```python
import functools

import numpy as np
import jax
import jax.numpy as jnp
from jax import lax
from jax.experimental import pallas as pl
from jax.experimental.pallas import tpu as pltpu

F32 = jnp.float32
BF16 = jnp.bfloat16

EPS = 1e-6
CHUNK = 64
TOPK_MAX = 256
ROPE_THETA = 10000.0
A_HEAD_DIM = 128
A_HEADS = 8
A_KV_HEADS = 2
IDX_HEADS = 16
IDX_DIM = 64
IDX_W_SCALE = (IDX_HEADS ** -0.5) * (IDX_DIM ** -0.5)
B_HEADS = 4
B_KEY_DIM = 64
B_VAL_DIM = 128
B_GATE_RANK = 16
B_GATE_TEMP = 16.0
C_WINDOWS = (2, 4, 8, 16)
C_WIDTH = 512
C_GROUP = 128
POOL_HIST = 15
HIST_ROWS = 16

LANES = 128
VMEM_LIMIT = 60 * 1024 * 1024

INT_MIN = -(2 ** 31)
IDX_SUB = 256
SEARCH_CLASSES = 8
ATT_KB = (2048, 512)
Q_SCALE = (A_HEAD_DIM ** -0.5) * float(np.log2(np.e))
BOUND_MARGIN = 1.02
BOUND_SAFE = 60.0

_SRC_SIZES = (("a_q", 1024), ("a_k", 256), ("a_v", 256), ("a_qi", 1024), ("a_ki", 64), ("a_wi", 16),
              ("b_q", 256), ("b_k", 256), ("b_v", 512), ("b_a", 16), ("b_g", 512), ("c_u", 512))
_PACK_ORDER = ("a_q", "a_qi", "b_v", "b_g", "c_u", "a_k", "a_v", "b_q", "b_k", "a_ki", "a_wi", "b_a")


def _build_layout():
    src, off = {}, 0
    for name, w in _SRC_SIZES:
        src[name] = (off, w)
        off += w
    lay, poff = {}, 0
    for name in _PACK_ORDER:
        so, w = src[name]
        pw = -(-w // LANES) * LANES
        lay[name] = (so, w, poff, pw)
        poff += pw
    return lay, poff


_LAYOUT, PACKED_WIDTH = _build_layout()


def _seg(name):
    _, _, poff, pw = _LAYOUT[name]
    return poff, pw


def _dot(a, b):
    return jnp.dot(a, b, preferred_element_type=F32)


def _dot_nt(a, b):
    return lax.dot_general(a, b, (((1,), (1,)), ((), ())), preferred_element_type=F32)


def _dot_tn(a, b):
    return lax.dot_general(a, b, (((0,), (0,)), ((), ())), preferred_element_type=F32)


def _bit_transpose32(rows):
    a = list(rows)
    j, m = 16, 0x0000FFFF
    while j:
        k = 0
        while k < 32:
            t = (a[k] ^ lax.shift_right_logical(a[k + j], jnp.int32(j))) & jnp.int32(m)
            a[k] = a[k] ^ t
            a[k + j] = a[k + j] ^ lax.shift_left(t, jnp.int32(j))
            k = (k + j + 1) & ~j
        j >>= 1
        m = (m ^ (m << j)) & 0xFFFFFFFF
    return a


def _cparams(sem):
    return pltpu.CompilerParams(dimension_semantics=sem, vmem_limit_bytes=VMEM_LIMIT)


def _norm_matmul_kernel(x_ref, g_ref, w_ref, o_ref, xn_ref):
    @pl.when(pl.program_id(1) == 0)
    def _():
        x = x_ref[...]
        ms = jnp.mean(x * x, axis=-1, keepdims=True)
        xn_ref[...] = (x * lax.rsqrt(ms + EPS) * g_ref[...]).astype(BF16)

    o_ref[...] = _dot(xn_ref[...], w_ref[...])


def _norm_matmul(x, g, w, bm, bn):
    M, D = x.shape
    N = w.shape[1]
    return pl.pallas_call(
        _norm_matmul_kernel,
        out_shape=jax.ShapeDtypeStruct((M, N), F32),
        grid=(M // bm, N // bn),
        in_specs=[pl.BlockSpec((bm, D), lambda i, j: (i, 0)),
                  pl.BlockSpec((1, D), lambda i, j: (0, 0)),
                  pl.BlockSpec((D, bn), lambda i, j: (0, j))],
        out_specs=pl.BlockSpec((bm, bn), lambda i, j: (i, j)),
        scratch_shapes=[pltpu.VMEM((bm, D), BF16)],
        compiler_params=_cparams(("arbitrary", "arbitrary")),
        name="norm_matmul",
    )(x, g.reshape(1, D), w)


def _post_kernel(p_ref, cosa_ref, sina_ref, cosb_ref, sinlo_ref, sinhi_ref, gq_ref, gk_ref, gki_ref,
                 walpha_ref, balpha_ref, poolw_ref, pscale_ref, hist0_ref,
                 q_ref, qi_ref, k32_ref, kb_ref, vb_ref, ki32_ref, kieo_ref, wt_ref,
                 bq_ref, bk_ref, bv_ref, lg_ref, gate_ref, oc_ref, ksq_ref,
                 uext_ref, hist_ref, *, tt, pos0):
    i = pl.program_id(1)

    def seg(name):
        off, w = _seg(name)
        return p_ref[:, off:off + w]

    cosa, sina = cosa_ref[...], sina_ref[...]

    def head_norm_rope(a, g):
        r = lax.rsqrt(jnp.mean(a * a, axis=-1, keepdims=True) + EPS)
        y = a * r * g
        return y * cosa + pltpu.roll(y, 64, 1) * sina

    aq = seg("a_q")
    gq = gq_ref[...]
    for h in range(A_HEADS):
        sl = slice(h * LANES, (h + 1) * LANES)
        q_ref[:, sl] = (head_norm_rope(aq[:, sl], gq) * Q_SCALE).astype(BF16)
    ak = seg("a_k")
    gk = gk_ref[...]
    ksq = jnp.zeros((tt, 1), F32)
    for h in range(A_KV_HEADS):
        sl = slice(h * LANES, (h + 1) * LANES)
        kk = head_norm_rope(ak[:, sl], gk)
        k32_ref[:, sl] = kk
        kb_ref[:, sl] = kk.astype(BF16)
        ksq = jnp.maximum(ksq, jnp.sum(kk * kk, axis=-1, keepdims=True))
    ksq_ref[...] = jnp.broadcast_to(jnp.max(ksq, axis=0, keepdims=True), ksq_ref.shape)
    vb_ref[...] = seg("a_v").astype(BF16)

    cosb, sinlo, sinhi = cosb_ref[...], sinlo_ref[...], sinhi_ref[...]

    def rope64(y):
        return y * cosb + pltpu.roll(y, 96, 1) * sinlo + pltpu.roll(y, 32, 1) * sinhi

    aqi = seg("a_qi")
    for gidx in range(IDX_HEADS // 2):
        sl = slice(gidx * LANES, (gidx + 1) * LANES)
        qi_ref[:, sl] = rope64(aqi[:, sl]).astype(BF16)
    aki = seg("a_ki")
    rki = lax.rsqrt(jnp.sum(aki * aki, axis=-1, keepdims=True) * (1.0 / IDX_DIM) + EPS)
    ki = rope64(aki * rki * gki_ref[...])
    ki32_ref[...] = ki[:, :IDX_DIM]
    kieo_ref[:, 0:LANES] = ki.astype(BF16)
    kieo_ref[:, LANES:2 * LANES] = pltpu.roll(ki, 64, 1).astype(BF16)

    awi = seg("a_wi") * IDX_W_SCALE
    if tt % LANES == 0:
        for c in range(tt // LANES):
            wt_ref[:, c * LANES:(c + 1) * LANES] = awi[c * LANES:(c + 1) * LANES, :].T[:IDX_HEADS, :]
    else:
        padded = jnp.concatenate([awi, jnp.zeros((LANES - tt, LANES), F32)], axis=0)
        wt_ref[...] = padded.T[:IDX_HEADS, :tt]

    bq_ref[...] = seg("b_q") * (B_KEY_DIM ** -0.5)
    bk_ref[...] = seg("b_k")
    bv_ref[...] = seg("b_v").astype(BF16)
    z = _dot(seg("b_a").astype(BF16), walpha_ref[...]) + balpha_ref[...]
    lg_ref[...] = (jnp.minimum(z, 0.0) - jnp.log1p(jnp.exp(-jnp.abs(z)))) * (1.0 / B_GATE_TEMP)
    bg = seg("b_g")
    gate_ref[...] = bg * (1.0 / (1.0 + jnp.exp(-bg)))

    @pl.when(i == 0)
    def _():
        hist_ref[...] = hist0_ref[...]

    u = seg("c_u")
    uext_ref[0:HIST_ROWS, :] = hist_ref[...]
    uext_ref[HIST_ROWS:HIST_ROWS + tt, :] = u
    hist_ref[...] = uext_ref[tt:tt + HIST_ROWS, :]
    pos = pos0 + i * tt + lax.broadcasted_iota(jnp.int32, (tt, 1), 0)
    for gi, w in enumerate(C_WINDOWS):
        sl = slice(gi * C_GROUP, (gi + 1) * C_GROUP)
        s = uext_ref[:, sl]
        shift = 1
        while shift < w:
            s = s + pltpu.roll(s, shift, 0)
            shift *= 2
        cnt = jnp.minimum(w, pos + 1).astype(F32)
        d = s[HIST_ROWS:, :] / cnt - u[:, sl]
        y = _dot(d.astype(BF16), poolw_ref[gi]) * pscale_ref[:, sl]
        oc_ref[:, sl] = y.astype(BF16)


def _post(P, tabs, gq, gk, gki, w_alpha, b_alpha, pool_w, pool_scale, hist0, tt, pos0):
    B, T, W = P.shape
    nT = T // tt
    tok = lambda c: pl.BlockSpec((None, tt, c), lambda b, i: (b, i, 0))
    tab = pl.BlockSpec((tt, LANES), lambda b, i: (i, 0))
    full = lambda shape: pl.BlockSpec(shape, lambda b, i: (0,) * len(shape))
    out_shape = [
        jax.ShapeDtypeStruct((B, T, 1024), BF16),
        jax.ShapeDtypeStruct((B, T, 1024), BF16),
        jax.ShapeDtypeStruct((B, T, 256), F32),
        jax.ShapeDtypeStruct((B, T, 256), BF16),
        jax.ShapeDtypeStruct((B, T, 256), BF16),
        jax.ShapeDtypeStruct((B, T, IDX_DIM), F32),
        jax.ShapeDtypeStruct((B, T, 256), BF16),
        jax.ShapeDtypeStruct((B, IDX_HEADS, T), F32),
        jax.ShapeDtypeStruct((B, T, 256), F32),
        jax.ShapeDtypeStruct((B, T, 256), F32),
        jax.ShapeDtypeStruct((B, T, 512), BF16),
        jax.ShapeDtypeStruct((B, T, 256), F32),
        jax.ShapeDtypeStruct((B, T, 512), F32),
        jax.ShapeDtypeStruct((B, T, 512), BF16),
        jax.ShapeDtypeStruct((B, nT, 8, LANES), F32),
    ]
    out_specs = [tok(1024), tok(1024), tok(256), tok(256), tok(256), tok(IDX_DIM), tok(256),
                 pl.BlockSpec((None, IDX_HEADS, tt), lambda b, i: (b, 0, i)),
                 tok(256), tok(256), tok(512), tok(256), tok(512), tok(512),
                 pl.BlockSpec((None, None, 8, LANES), lambda b, i: (b, i, 0, 0))]
    return pl.pallas_call(
        functools.partial(_post_kernel, tt=tt, pos0=pos0),
        out_shape=out_shape,
        grid=(B, nT),
        in_specs=[tok(W), tab, tab, tab, tab, tab,
                  full((1, LANES)), full((1, LANES)), full((1, LANES)),
                  full((LANES, 256)), full((1, 256)),
                  full((len(C_WINDOWS), C_GROUP, C_GROUP)), full((1, C_WIDTH)),
                  pl.BlockSpec((None, HIST_ROWS, C_WIDTH), lambda b, i: (b, 0, 0))],
        out_specs=out_specs,
        scratch_shapes=[pltpu.VMEM((tt + HIST_ROWS, C_WIDTH), F32), pltpu.VMEM((HIST_ROWS, C_WIDTH), F32)],
        compiler_params=_cparams(("arbitrary", "arbitrary")),
        name="post",
    )(P, *tabs, gq, gk, gki, w_alpha, b_alpha, pool_w, pool_scale, hist0)


def _attn_kernel(*refs, tq, kb_sizes, causal, n_keys, topk, has_bound):
    q_ref, qi_ref, wt_ref, k_ref, v_ref, ki_ref = refs[:6]
    ksq_ref = refs[6] if has_bound else None
    o_ref, key_ref, plane_ref, acc_ref, m_ref, l_ref = refs[-6:]
    i = pl.program_id(1)

    @pl.when(i == 0)
    def _():
        plane_ref[...] = jnp.zeros(plane_ref.shape, jnp.int32)

    n_slabs = IDX_HEADS // 2
    kb_big, kb_small = kb_sizes
    if causal:
        n_vis = (i + 1) * tq
        t_abs = i * tq + lax.broadcasted_iota(jnp.int32, (1, tq), 1)
        limit = (t_abs // CHUNK + 1) * CHUNK
    else:
        n_vis = n_keys
        limit = jnp.full((1, tq), n_keys, jnp.int32)
    n_big = (n_vis - 1) // kb_big
    n_small = (n_vis - n_big * kb_big + kb_small - 1) // kb_small
    n_rows = n_big * kb_big + n_small * kb_small

    def for_blocks(body, carry):
        def run(size, first, count, c):
            return lax.fori_loop(0, count, lambda j, cc: body(pl.multiple_of(first + j * size, size), size, cc), c)

        return run(kb_small, n_big * kb_big, n_small, run(kb_big, 0, n_big, carry))

    q2 = jnp.concatenate([qi_ref[:, g * LANES:(g + 1) * LANES] for g in range(n_slabs)], axis=0)

    def idx_body(off0, size, carry):
        for sb in range(size // IDX_SUB):
            off = pl.multiple_of(off0 + sb * IDX_SUB, IDX_SUB)
            ye = _dot_nt(ki_ref[pl.ds(off, IDX_SUB), 0:LANES], q2)
            yo = _dot_nt(ki_ref[pl.ds(off, IDX_SUB), LANES:2 * LANES], q2)
            score = jnp.zeros((IDX_SUB, tq), F32)
            for g in range(n_slabs):
                score = score + jnp.maximum(ye[:, g * tq:(g + 1) * tq], 0.0) * wt_ref[2 * g:2 * g + 1, :]
                score = score + jnp.maximum(yo[:, g * tq:(g + 1) * tq], 0.0) * wt_ref[2 * g + 1:2 * g + 2, :]
            bits = pltpu.bitcast(score, jnp.int32)
            key = bits ^ ((bits >> 31) & jnp.int32(0x7FFFFFFF))
            if size == kb_small:
                s_idx = off + lax.broadcasted_iota(jnp.int32, (IDX_SUB, 1), 0)
                key = jnp.where(s_idx < limit, key, jnp.int32(INT_MIN))
            key_ref[pl.ds(off, IDX_SUB), :] = key
            ukey = key ^ jnp.int32(INT_MIN)
            planes = _bit_transpose32([ukey[8 * v:8 * v + 8, :] for v in range(32)])
            grp = pl.multiple_of(off // 32, 8)
            for p in range(32):
                plane_ref[p, pl.ds(grp, 8), :] = planes[p]
        return carry

    for_blocks(idx_body, 0)

    n_words = key_ref.shape[0] // 32

    def popcount_rows(x):
        part = jnp.sum(lax.population_count(x).reshape(x.shape[0] // 8, 8, tq), axis=0)
        return jnp.sum(part, axis=0, keepdims=True)

    def search(nw):
        def run():
            word_row = lax.broadcasted_iota(jnp.int32, (nw, tq), 0)
            in_play = jnp.where(word_row < n_rows // 32, jnp.int32(-1), jnp.int32(0))

            def bit_body(p, carry):
                cand, above, ukth = carry
                plane = plane_ref[p, pl.ds(0, nw), :]
                ones = cand & plane
                c1 = popcount_rows(ones)
                take = above + c1 >= topk
                cand = jnp.where(take, ones, cand & ~plane)
                above = jnp.where(take, above, above + c1)
                ukth = jnp.where(take, ukth | lax.shift_left(jnp.int32(1), 31 - p), ukth)
                return cand, above, ukth

            zero_row = jnp.zeros((1, tq), jnp.int32)
            at_kth, above, ukth = lax.fori_loop(0, 32, bit_body, (in_play, zero_row, zero_row))
            return above, ukth, popcount_rows(at_kth)

        return run

    if causal and n_words % (8 * SEARCH_CLASSES) == 0:
        step = n_words // SEARCH_CLASSES
        cls = (n_rows // 32 + step - 1) // step - 1
        above, ukth, n_at = lax.switch(cls, [search(step * (c + 1)) for c in range(SEARCH_CLASSES)])
    else:
        above, ukth, n_at = search(n_words)()
    tau = jnp.maximum(ukth ^ jnp.int32(INT_MIN), jnp.int32(INT_MIN + 1))

    def count32(pred):
        def body(off, size, acc):
            s_idx = off + lax.broadcasted_iota(jnp.int32, (size, 1), 0)
            hit = jnp.where(pred(key_ref[pl.ds(off, size), :], s_idx), 1, 0)
            return acc + jnp.sum(hit.reshape(size // 8, 8, tq), axis=0)

        return jnp.sum(for_blocks(body, jnp.zeros((8, tq), jnp.int32)), axis=0, keepdims=True)

    n_ge = jnp.where(ukth != 0, above + n_at, 0)

    @pl.when(jnp.max(n_ge) > topk)
    def _():
        quota = topk - count32(lambda k, s: k > tau)
        idx_bits = int(key_ref.shape[0]).bit_length()

        def bit_body(bi, cut):
            cand = cut | lax.shift_left(jnp.int32(1), idx_bits - 1 - bi)
            kept = count32(lambda k, s: jnp.where(k == tau, s, cand) < cand)
            return jnp.where(kept <= quota, cand, cut)

        cut = lax.fori_loop(0, idx_bits, bit_body, jnp.zeros((1, tq), jnp.int32))

        def demote(off, size, carry):
            s_idx = off + lax.broadcasted_iota(jnp.int32, (size, 1), 0)
            blk = key_ref[pl.ds(off, size), :]
            key_ref[pl.ds(off, size), :] = jnp.where(jnp.where(blk == tau, s_idx, -1) >= cut, tau - 1, blk)
            return carry

        for_blocks(demote, 0)

    hpg = A_HEADS // A_KV_HEADS
    qgs = [jnp.concatenate([q_ref[:, (g * hpg + h) * LANES:(g * hpg + h + 1) * LANES] for h in range(hpg)], axis=0)
           for g in range(A_KV_HEADS)]
    l_ref[...] = jnp.zeros(l_ref.shape, F32)
    acc_ref[...] = jnp.zeros(acc_ref.shape, F32)

    def masked_scores(off, size, g, sel, fill, shift):
        s = _dot_nt(k_ref[pl.ds(off, size), g * LANES:(g + 1) * LANES], qgs[g])
        return jnp.concatenate([jnp.where(sel, shift(s[:, h * tq:(h + 1) * tq], h), fill) for h in range(hpg)], axis=1)

    def add_pv(off, size, g, p):
        return _dot_tn(v_ref[pl.ds(off, size), g * LANES:(g + 1) * LANES], p.astype(BF16))

    def online():
        m_ref[...] = jnp.full(m_ref.shape, -1e30, F32)

        def body(off, size, carry):
            sel = key_ref[pl.ds(off, size), :] >= tau
            for g in range(A_KV_HEADS):
                s = masked_scores(off, size, g, sel, -jnp.inf, lambda x, h: x)
                m_old = m_ref[g]
                m_new = jnp.maximum(m_old, jnp.max(s, axis=0, keepdims=True))
                alpha = jnp.exp2(m_old - m_new)
                p = jnp.exp2(s - m_new)
                l_ref[g] = alpha * l_ref[g] + jnp.sum(p, axis=0, keepdims=True)
                acc_ref[g] = acc_ref[g] * alpha + add_pv(off, size, g, p)
                m_ref[g] = m_new
            return carry

        for_blocks(body, 0)

    def bounded(bounds):
        def body(off, size, carry):
            sel = key_ref[pl.ds(off, size), :] >= tau
            for g in range(A_KV_HEADS):
                b = bounds[g]
                p = masked_scores(off, size, g, sel, 0.0, lambda x, h: jnp.exp2(x - b[:, h * tq:(h + 1) * tq]))
                l_ref[g] = l_ref[g] + jnp.sum(p, axis=0, keepdims=True)
                acc_ref[g] = acc_ref[g] + add_pv(off, size, g, p)
            return carry

        for_blocks(body, 0)

    if ksq_ref is None:
        online()
    else:
        kmax = jnp.sqrt(jnp.max(ksq_ref[...]))
        ones = jnp.ones((8, LANES), BF16)
        bounds = [jnp.sqrt(_dot_nt(ones, qg * qg)[0:1, :]) * (kmax * BOUND_MARGIN) for qg in qgs]
        safe = jnp.max(jnp.maximum(bounds[0], bounds[1])) <= BOUND_SAFE

        @pl.when(safe)
        def _():
            bounded(bounds)

        @pl.when(jnp.logical_not(safe))
        def _():
            online()

    for g in range(A_KV_HEADS):
        o = acc_ref[g] / l_ref[g]
        for h in range(hpg):
            hh = g * hpg + h
            o_ref[:, hh * LANES:(hh + 1) * LANES] = o[:, h * tq:(h + 1) * tq].T.astype(o_ref.dtype)


def _attn(q, qi, wt, kb, vb, kieo, ksq, tq, causal, n_keys, topk):
    B, T, _ = q.shape
    S = kb.shape[1]
    assert S % ATT_KB[1] == 0 and ATT_KB[0] % ATT_KB[1] == 0 and ATT_KB[1] % IDX_SUB == 0
    hpg = A_HEADS // A_KV_HEADS
    kern = functools.partial(_attn_kernel, tq=tq, kb_sizes=ATT_KB, causal=causal, n_keys=n_keys, topk=topk,
                             has_bound=ksq is not None)
    per_batch = pl.BlockSpec((None, S, 256), lambda b, i: (b, 0, 0))
    in_specs = [pl.BlockSpec((None, tq, 1024), lambda b, i: (b, i, 0)),
                pl.BlockSpec((None, tq, 1024), lambda b, i: (b, i, 0)),
                pl.BlockSpec((None, IDX_HEADS, tq), lambda b, i: (b, 0, i)),
                per_batch, per_batch, per_batch]
    args = [q, qi, wt, kb, vb, kieo]
    if ksq is not None:
        in_specs.append(pl.BlockSpec((None,) + ksq.shape[1:], lambda b, i: (b, 0, 0, 0)))
        args.append(ksq)
    return pl.pallas_call(
        kern,
        out_shape=jax.ShapeDtypeStruct((B, T, 1024), BF16),
        grid=(B, T // tq),
        in_specs=in_specs,
        out_specs=pl.BlockSpec((None, tq, 1024), lambda b, i: (b, i, 0)),
        scratch_shapes=[pltpu.VMEM((S, tq), jnp.int32),
                        pltpu.VMEM((32, S // 32, tq), jnp.int32),
                        pltpu.VMEM((A_KV_HEADS, A_HEAD_DIM, hpg * tq), F32),
                        pltpu.VMEM((A_KV_HEADS, 1, hpg * tq), F32),
                        pltpu.VMEM((A_KV_HEADS, 1, hpg * tq), F32)],
        compiler_params=_cparams(("arbitrary", "arbitrary")),
        name="attn",
    )(*args)


def _gla_consts(C):
    nl = int(np.log2(C))
    M = np.zeros((nl + 2, C, C), np.float32)
    masks = np.zeros((nl + 1, C, C), np.float32)
    for j in range(nl):
        m = 1 << j
        for t in range(C):
            blk = (t // (2 * m)) * 2 * m
            mid = blk + m
            if t >= mid:
                M[j, t, mid:t + 1] = 1
                masks[j, t, blk:mid] = 1
            else:
                M[j, t, t + 1:mid] = 1
    for t in range(C):
        M[nl, t, :t + 1] = 1
        M[nl + 1, t, t + 1:] = 1
    masks[nl] = np.eye(C)
    return M.reshape((nl + 2) * C, C), np.concatenate([masks, masks], axis=1)


def _gla_kernel(q_ref, k_ref, v_ref, lg_ref, gate_ref, mcat_ref, masks_ref, gn_ref, s0_ref,
                o_ref, sout_ref, s_ref, *, tt, C):
    i = pl.program_id(1)
    nl = int(np.log2(C))

    @pl.when(i == 0)
    def _():
        s_ref[...] = s0_ref[...]

    lane = lax.broadcasted_iota(jnp.int32, (1, LANES), 1)
    low = lane < B_KEY_DIM
    row = lax.broadcasted_iota(jnp.int32, (C, 1), 0)
    rr = lax.broadcasted_iota(jnp.int32, (LANES, 2 * B_VAL_DIM), 0)
    cc = lax.broadcasted_iota(jnp.int32, (LANES, 2 * B_VAL_DIM), 1)
    blockdiag = (rr < B_KEY_DIM) == (cc < B_VAL_DIM)
    gn = gn_ref[...]

    def stack2(x):
        return jnp.concatenate([jnp.where(low, x, 0.0), jnp.where(low, 0.0, x)], axis=0).astype(BF16)

    def chunk_body(c, carry):
        r0 = pl.multiple_of(c * C, C)
        qc = q_ref[pl.ds(r0, C), :]
        kc = k_ref[pl.ds(r0, C), :]
        vc = v_ref[pl.ds(r0, C), :]
        lgc = lg_ref[pl.ds(r0, C), :]
        hi = lgc.astype(BF16)
        r1 = lgc - hi.astype(F32)
        mid = r1.astype(BF16)
        lo = (r1 - mid.astype(F32)).astype(BF16)
        mcat = mcat_ref[...]
        E = jnp.exp(_dot(mcat, hi) + _dot(mcat, mid) + _dot(mcat, lo))
        e_b = E[nl * C:(nl + 1) * C, :]
        e_k = E[(nl + 1) * C:(nl + 2) * C, :]
        qt = qc * e_b
        kt = kc * e_k
        e_last = e_b[C - 1:C, :]
        for g in range(B_HEADS // 2):
            sl = slice(g * LANES, (g + 1) * LANES)
            att = _dot_nt(stack2(qc[:, sl]), kc[:, sl].astype(BF16)) * masks_ref[nl]
            for j in range(nl):
                right = ((row >> j) & 1) == 1
                x = jnp.where(right, qc[:, sl], kc[:, sl]) * E[j * C:(j + 1) * C, sl]
                att = att + _dot_nt(stack2(x), x.astype(BF16)) * masks_ref[j]
            att = att.astype(BF16)
            vg = vc[:, g * 2 * B_VAL_DIM:(g + 1) * 2 * B_VAL_DIM]
            sg = s_ref[g]
            o_intra = jnp.concatenate([_dot(att[0:C], vg[:, 0:B_VAL_DIM]),
                                       _dot(att[C:2 * C], vg[:, B_VAL_DIM:2 * B_VAL_DIM])], axis=1)
            o = o_intra + _dot(qt[:, sl].astype(BF16), sg.astype(BF16))
            upd = _dot_tn(kt[:, sl].astype(BF16), vg)
            dec = jnp.broadcast_to(e_last[:, sl], (LANES, LANES)).T
            dec = jnp.concatenate([dec, dec], axis=1)
            s_ref[g] = dec * sg + jnp.where(blockdiag, upd, 0.0)
            for hh in range(2):
                h = 2 * g + hh
                oh = o[:, hh * B_VAL_DIM:(hh + 1) * B_VAL_DIM]
                r = lax.rsqrt(jnp.mean(oh * oh, axis=-1, keepdims=True) + EPS)
                gsl = slice(h * B_VAL_DIM, (h + 1) * B_VAL_DIM)
                o_ref[pl.ds(r0, C), gsl] = (oh * r * gn * gate_ref[pl.ds(r0, C), gsl]).astype(o_ref.dtype)
        return carry

    n_chunks = tt // C
    per_step = 4 if n_chunks % 4 == 0 else 1

    def step(j, carry):
        for u in range(per_step):
            chunk_body(j * per_step + u, carry)
        return carry

    lax.fori_loop(0, n_chunks // per_step, step, 0)

    @pl.when(i == pl.num_programs(1) - 1)
    def _():
        sout_ref[...] = s_ref[...]


def _gla(bq, bk, bv, lg, gate, gn, s0, tt, C):
    B, T, _ = bq.shape
    mcat, masks = _gla_consts(C)
    nl = int(np.log2(C))
    tok = lambda c: pl.BlockSpec((None, tt, c), lambda b, i: (b, i, 0))
    full = lambda shape: pl.BlockSpec(shape, lambda b, i: (0,) * len(shape))
    st = pl.BlockSpec((None, 2, LANES, 2 * B_VAL_DIM), lambda b, i: (b, 0, 0, 0))
    return pl.pallas_call(
        functools.partial(_gla_kernel, tt=tt, C=C),
        out_shape=[jax.ShapeDtypeStruct((B, T, 512), BF16),
                   jax.ShapeDtypeStruct((B, 2, LANES, 2 * B_VAL_DIM), F32)],
        grid=(B, T // tt),
        in_specs=[tok(256), tok(256), tok(512), tok(256), tok(512),
                  full(((nl + 2) * C, C)), full((nl + 1, 2 * C, C)), full((1, B_VAL_DIM)), st],
        out_specs=[tok(512), st],
        scratch_shapes=[pltpu.VMEM((2, LANES, 2 * B_VAL_DIM), F32)],
        compiler_params=_cparams(("arbitrary", "arbitrary")),
        name="gla",
    )(bq, bk, bv, lg, gate, jnp.asarray(mcat, BF16), jnp.asarray(masks, F32), gn, s0)


def _out_kernel(x_ref, oa_ref, ob_ref, oc_ref, wo_ref, g2_ref, h_ref, m_ref):
    acc = _dot(oa_ref[...], wo_ref[0:1024, :])
    acc = acc + _dot(ob_ref[...], wo_ref[1024:1536, :])
    acc = acc + _dot(oc_ref[...], wo_ref[1536:2048, :])
    h = x_ref[...] + acc
    h_ref[...] = h
    ms = jnp.mean(h * h, axis=-1, keepdims=True)
    m_ref[...] = (h * lax.rsqrt(ms + EPS) * g2_ref[...]).astype(BF16)


def _out_proj(x, oa, ob, oc, wo, g2, bm):
    M, D = x.shape
    row = lambda c: pl.BlockSpec((bm, c), lambda i: (i, 0))
    return pl.pallas_call(
        _out_kernel,
        out_shape=[jax.ShapeDtypeStruct((M, D), F32), jax.ShapeDtypeStruct((M, D), BF16)],
        grid=(M // bm,),
        in_specs=[row(D), row(1024), row(512), row(512),
                  pl.BlockSpec((D, D), lambda i: (0, 0)), pl.BlockSpec((1, D), lambda i: (0, 0))],
        out_specs=[row(D), row(D)],
        compiler_params=_cparams(("arbitrary",)),
        name="out_proj",
    )(x, oa, ob, oc, wo, g2.reshape(1, D))


def _ffn_kernel(h_ref, m_ref, wg_ref, wu_ref, wd_ref, y_ref):
    @pl.when(pl.program_id(1) == 0)
    def _():
        y_ref[...] = h_ref[...]

    m = m_ref[...]
    gate = _dot(m, wg_ref[...])
    up = _dot(m, wu_ref[...])
    a = (gate * (1.0 / (1.0 + jnp.exp(-gate))) * up).astype(BF16)
    y_ref[...] += _dot(a, wd_ref[...])


def _ffn(h, m, wg, wu, wd, bm, bf):
    M, D = h.shape
    Fh = wg.shape[1]
    return pl.pallas_call(
        _ffn_kernel,
        out_shape=jax.ShapeDtypeStruct((M, D), F32),
        grid=(M // bm, Fh // bf),
        in_specs=[pl.BlockSpec((bm, D), lambda i, j: (i, 0)),
                  pl.BlockSpec((bm, D), lambda i, j: (i, 0)),
                  pl.BlockSpec((D, bf), lambda i, j: (0, j)),
                  pl.BlockSpec((D, bf), lambda i, j: (0, j)),
                  pl.BlockSpec((bf, D), lambda i, j: (j, 0))],
        out_specs=pl.BlockSpec((bm, D), lambda i, j: (i, 0)),
        compiler_params=_cparams(("arbitrary", "arbitrary")),
        name="ffn",
    )(h, m, wg, wu, wd)


def _rope_tables(pos0, T):
    pos = (pos0 + jnp.arange(T, dtype=jnp.int32)).astype(F32)

    def cs(half):
        inv = ROPE_THETA ** (-jnp.arange(half, dtype=F32) / half)
        ang = pos[:, None] * inv[None, :]
        return jnp.cos(ang), jnp.sin(ang)

    c64, s64 = cs(A_HEAD_DIM // 2)
    c32, s32 = cs(IDX_DIM // 2)
    z32 = jnp.zeros_like(s32)
    return (jnp.concatenate([c64, c64], axis=1), jnp.concatenate([-s64, s64], axis=1),
            jnp.concatenate([c32, c32, c32, c32], axis=1),
            jnp.concatenate([-s32, z32, -s32, z32], axis=1),
            jnp.concatenate([z32, s32, z32, s32], axis=1))


def _pack_w_in(w_in):
    cols = []
    for name in _PACK_ORDER:
        so, w, _, pw = _LAYOUT[name]
        piece = w_in[:, :, so:so + w]
        if pw > w:
            piece = jnp.pad(piece, ((0, 0), (0, 0), (0, pw - w)))
        cols.append(piece)
    return jnp.concatenate(cols, axis=-1).astype(BF16)


def _state_to_groups(s):
    B = s.shape[0]
    s = s.reshape(B, 2, 2, B_KEY_DIM, B_VAL_DIM)
    z = jnp.zeros_like(s[:, :, 0])
    top = jnp.concatenate([s[:, :, 0], z], axis=-1)
    bot = jnp.concatenate([z, s[:, :, 1]], axis=-1)
    return jnp.concatenate([top, bot], axis=-2)


def _groups_to_state(sg):
    B = sg.shape[0]
    h0 = sg[:, :, :B_KEY_DIM, :B_VAL_DIM]
    h1 = sg[:, :, B_KEY_DIM:, B_VAL_DIM:]
    return jnp.stack([h0, h1], axis=2).reshape(B, B_HEADS, B_KEY_DIM, B_VAL_DIM)


def _layer(x, pos0, cache, prm):
    (g1, w_in_p, gq, gk, gki, w_alpha, b_alpha, g_gla, pool_w, pool_scale, w_o, g2, w_gate, w_up, w_down) = prm
    B, T, D = x.shape
    M = B * T
    x2 = x.reshape(M, D)
    bm = min(512, M)
    P = _norm_matmul(x2, g1, w_in_p, min(1024, M), PACKED_WIDTH // 3).reshape(B, T, PACKED_WIDTH)

    tt = min(512, T)
    tabs = _rope_tables(pos0, T)
    gki_p = jnp.pad(gki, (0, LANES - IDX_DIM)).reshape(1, LANES)
    w_alpha_p = jnp.pad(w_alpha, ((0, LANES - B_GATE_RANK), (0, 0))).astype(BF16)
    if cache is None:
        hist0 = jnp.zeros((B, HIST_ROWS, C_WIDTH), F32)
    else:
        hist0 = jnp.pad(cache[4], ((0, 0), (HIST_ROWS - POOL_HIST, 0), (0, 0)))
    (q, qi, k32, kb, vb, ki32, kieo, wt, bq, bk, bv, lg, gate, oc, ksq) = _post(
        P, tabs, gq.reshape(1, LANES), gk.reshape(1, LANES), gki_p, w_alpha_p, b_alpha.reshape(1, -1),
        pool_w.astype(BF16), pool_scale.reshape(1, -1), hist0, tt, pos0)

    v_off, _ = _seg("a_v")
    v32 = P[:, :, v_off:v_off + 256]
    u_off, _ = _seg("c_u")
    pool_state = P[:, T - POOL_HIST:, u_off:u_off + C_WIDTH]

    if cache is None:
        tq = 2 * LANES
        oa = _attn(q, qi, wt, kb, vb, kieo, ksq, tq, True, T, min(TOPK_MAX, T // 4))
        s0 = jnp.zeros((B, 2, LANES, 2 * B_VAL_DIM), F32)
        ob, sg = _gla(bq, bk, bv, lg, gate, g_gla.reshape(1, -1), s0, tt, CHUNK)
    else:
        ck, cv, cki, c_gla, _ = cache
        past = ck.shape[1]
        L = past + T
        S = -(-L // ATT_KB[1]) * ATT_KB[1]
        padk = lambda a: jnp.pad(a, ((0, 0), (0, S - L), (0, 0)))
        k_all = padk(jnp.concatenate([ck.reshape(B, past, 256).astype(BF16), kb], axis=1))
        v_all = padk(jnp.concatenate([cv.reshape(B, past, 256).astype(BF16), vb], axis=1))
        zc = jnp.zeros((B, past, IDX_DIM), BF16)
        cki_b = cki.astype(BF16)
        ki_all = padk(jnp.concatenate([jnp.concatenate([cki_b, zc, zc, cki_b], axis=-1), kieo], axis=1))
        tq = LANES
        padq = lambda a: jnp.pad(a, ((0, 0), (0, tq - T), (0, 0)))
        wt_p = jnp.pad(wt, ((0, 0), (0, 0), (0, tq - T)))
        oa = _attn(padq(q), padq(qi), wt_p, k_all, v_all, ki_all, None, tq, False, L,
                   min(TOPK_MAX, L // 4))[:, :T]
        ob, sg = _gla(bq, bk, bv, lg, gate, g_gla.reshape(1, -1), _state_to_groups(c_gla), tt, T)

    h, m = _out_proj(x2, oa.reshape(M, -1), ob.reshape(M, -1), oc.reshape(M, -1), w_o, g2, bm)
    y = _ffn(h, m, w_gate, w_up, w_down, min(1024, M), 512)
    state = (k32.reshape(B, T, A_KV_HEADS, A_HEAD_DIM), v32.reshape(B, T, A_KV_HEADS, A_HEAD_DIM),
             ki32, _groups_to_state(sg), pool_state)
    return y.reshape(B, T, D), state


def kernel(x_prompt, x_sample, cache_k, cache_v, cache_kidx, state_gla, state_pool,
           norm1, w_in, q_norm, k_norm, kidx_norm, w_alpha, b_alpha, gla_norm, pool_w, pool_scale,
           w_o, norm2, w_gate, w_up, w_down):
    depth = w_in.shape[0]
    past = cache_k.shape[2]
    w_in_p = _pack_w_in(w_in)
    w_o_b, w_gate_b, w_up_b, w_down_b = (w.astype(BF16) for w in (w_o, w_gate, w_up, w_down))
    yp, ys = x_prompt, x_sample
    st_p, st_s = [], []
    for l in range(depth):
        prm = (norm1[l], w_in_p[l], q_norm[l], k_norm[l], kidx_norm[l], w_alpha[l], b_alpha[l], gla_norm[l],
               pool_w[l], pool_scale[l], w_o_b[l], norm2[l], w_gate_b[l], w_up_b[l], w_down_b[l])
        yp, sp = _layer(yp, 0, None, prm)
        ys, ss = _layer(ys, past, (cache_k[l], cache_v[l], cache_kidx[l], state_gla[l], state_pool[l]), prm)
        st_p.append(sp)
        st_s.append(ss)
    stk = lambda sts, i: jnp.stack([s[i] for s in sts], axis=0)
    return (yp, ys,
            stk(st_p, 0), stk(st_p, 1), stk(st_p, 2), stk(st_p, 3), stk(st_p, 4),
            stk(st_s, 0), stk(st_s, 1), stk(st_s, 2), stk(st_s, 3), stk(st_s, 4))
```
